```python
import jax, jax.numpy as jnp
from jax import lax
import numpy as np

D_MODEL = 1024
BATCH = 4
SEQ = 4096
DEPTH = 2
DEC_BATCH = 32
DEC_SEQ = 8
PAST_LEN = 16384
PAGE_SIZE = 128

HEAD_DIM = 64
A_HEADS = 8
B_HEADS = 4
M_HEADS = 4
IDX_HEADS = 8
IDX_DIM = 32
N_MEM = 256
MOBA_BLOCK = 256
MOBA_TOPK = 3
DSA_TOPK = 256
D_FF = 4 * D_MODEL
Q_BLOCK = 128
ALPHA = (2 * DEPTH) ** 0.25
BETA = (8 * DEPTH) ** -0.25
LN_EPS = 1e-5
ATTN_SCALE = HEAD_DIM ** -0.5
IDX_SCALE = IDX_DIM ** -0.5
IDX_W_SCALE = IDX_HEADS ** -0.5
A_W = A_HEADS * HEAD_DIM
B_W = B_HEADS * HEAD_DIM
M_W = M_HEADS * HEAD_DIM
SPLITS = (A_W, A_W, A_W, B_W, B_W, B_W, IDX_HEADS * IDX_DIM, IDX_DIM, IDX_HEADS, M_W)
PROJ_W = sum(SPLITS)
V_SLOTS = (2, 5)

kernel_name = 'hymba_moba_dsa_memory_deepnorm_step'


def _layer_norm(x, g, b):
    xf = x.astype(jnp.float32)
    mu = jnp.mean(xf, axis=-1, keepdims=True)
    var = jnp.mean(jnp.square(xf - mu), axis=-1, keepdims=True)
    y = (xf - mu) * lax.rsqrt(var + LN_EPS) * g.astype(jnp.float32) + b.astype(jnp.float32)
    return y.astype(x.dtype)


def _alibi_slopes(n):
    return jnp.exp2(-8.0 * jnp.arange(1, n + 1, dtype=jnp.float32) / n)


def _biased(s, dist, slope, ok):
    return jnp.where(ok, s.astype(jnp.float32) * ATTN_SCALE - slope * dist.astype(jnp.float32), -jnp.inf)


def _project(h, w):
    lead = h.shape[:-1]
    p = h @ w
    cuts = [int(c) for c in np.cumsum(SPLITS)[:-1]]
    qa, ka, va, qb, kb, vb, iq, ik, iw, qm = jnp.split(p, cuts, axis=-1)
    ah = (*lead, A_HEADS, HEAD_DIM)
    bh = (*lead, B_HEADS, HEAD_DIM)
    return (qa.reshape(ah), ka.reshape(ah), va.reshape(ah),
            qb.reshape(bh), kb.reshape(bh), vb.reshape(bh),
            iq.reshape(*lead, IDX_HEADS, IDX_DIM), ik, iw,
            qm.reshape(*lead, M_HEADS, HEAD_DIM))


def _mem_kv(mem, w):
    k, v = jnp.split(mem @ w, 2, axis=-1)
    shp = (*mem.shape[:-1], M_HEADS, HEAD_DIM)
    return k.reshape(shp), v.reshape(shp)


def _mem_attend(q, k, v):
    s = jnp.einsum('nthd,nmhd->nhtm', q, k).astype(jnp.float32) * ATTN_SCALE
    p = jax.nn.softmax(s, axis=-1)
    return jnp.einsum('nhtm,nmhd->nthd', p, v.astype(jnp.float32))


def _moba_attend(q, tpos, means, gather_blocks, k_own, v_own, own_pos, slopes):
    n_q = q.shape[0]
    dist_own = tpos[:, None, None] - own_pos[None, None, :]
    s_own = _biased(jnp.einsum('qhd,hkd->qhk', q, k_own), dist_own, slopes[None, :, None], dist_own >= 0)
    nb_full = means.shape[1]
    n_sel = min(MOBA_TOPK, nb_full)
    if n_sel == 0:
        p_own = jax.nn.softmax(s_own, axis=-1)
        return jnp.einsum('qhk,hkd->qhd', p_own, v_own.astype(jnp.float32))
    n_past = (tpos // MOBA_BLOCK)[:, None, None]
    gate = jnp.einsum('qhd,hnd->qhn', q.astype(jnp.float32), means)
    gate = jnp.where(jnp.arange(nb_full)[None, None, :] < n_past, gate, -jnp.inf)
    _, idx = lax.top_k(gate, n_sel)
    k_sel, v_sel = gather_blocks(idx)
    dist_sel = tpos[:, None, None, None] - (idx[..., None] * MOBA_BLOCK + jnp.arange(MOBA_BLOCK))
    ok = (idx < n_past)[..., None] & (dist_sel >= 0)
    s_sel = _biased(jnp.einsum('qhd,qhnkd->qhnk', q, k_sel), dist_sel, slopes[None, :, None, None], ok)
    n_flat = n_sel * MOBA_BLOCK
    p = jax.nn.softmax(jnp.concatenate([s_sel.reshape(n_q, A_HEADS, n_flat), s_own], axis=-1), axis=-1)
    p_sel = p[..., :n_flat].reshape(n_q, A_HEADS, n_sel, MOBA_BLOCK)
    return (jnp.einsum('qhnk,qhnkd->qhd', p_sel, v_sel.astype(jnp.float32))
            + jnp.einsum('qhk,hkd->qhd', p[..., n_flat:], v_own.astype(jnp.float32)))


def _moba_prompt(qa, ka, va, slopes):
    bsz, seq = qa.shape[:2]
    nb_full = seq // MOBA_BLOCK
    nb_ceil = -(-seq // MOBA_BLOCK)
    padw = ((0, 0), (0, nb_ceil * MOBA_BLOCK - seq), (0, 0), (0, 0))

    def blocks(a):
        return jnp.moveaxis(jnp.pad(a, padw).reshape(bsz, nb_ceil, MOBA_BLOCK, A_HEADS, HEAD_DIM), 3, 1)

    kh, vh = blocks(ka), blocks(va)
    means = jnp.mean(kh[:, :, :nb_full].astype(jnp.float32), axis=3)
    n_qb = seq // Q_BLOCK
    hsel = jnp.arange(A_HEADS)[None, :, None]
    blk_pos = jnp.arange(MOBA_BLOCK)

    def one(i):
        b = i // n_qb
        t0 = (i % n_qb) * Q_BLOCK
        tpos = t0 + jnp.arange(Q_BLOCK)
        q = lax.dynamic_slice(qa, (b, t0, 0, 0), (1, Q_BLOCK, A_HEADS, HEAD_DIM))[0]
        own = t0 // MOBA_BLOCK

        def gather_blocks(idx):
            return kh[b, hsel, idx], vh[b, hsel, idx]

        return _moba_attend(q, tpos, means[b], gather_blocks, kh[b, :, own], vh[b, :, own],
                            own * MOBA_BLOCK + blk_pos, slopes)

    out = lax.map(one, jnp.arange(bsz * n_qb))
    return out.reshape(bsz, seq, A_HEADS, HEAD_DIM)


def _moba_sample(qa, ka, va, cache_k, cache_v, page_table, l, slopes):
    n_tok = qa.shape[1]
    ppb = MOBA_BLOCK // PAGE_SIZE
    nb_full = PAST_LEN // MOBA_BLOCK
    own_start = nb_full * MOBA_BLOCK
    r_pages = (PAST_LEN - own_start) // PAGE_SIZE
    r_rows = r_pages * PAGE_SIZE
    tpos = PAST_LEN + jnp.arange(n_tok)
    own_pos = own_start + jnp.arange(r_rows + n_tok)
    hsel = jnp.arange(A_HEADS)[None, :, None, None, None]
    off = jnp.arange(PAGE_SIZE)

    def one(args):
        q, kn, vn, pt = args
        if nb_full > 0:
            kp = cache_k[l, pt[:nb_full * ppb]].reshape(nb_full, MOBA_BLOCK, A_HEADS, HEAD_DIM)
            means = jnp.moveaxis(jnp.mean(kp.astype(jnp.float32), axis=1), 0, 1)
        else:
            means = jnp.zeros((A_HEADS, 0, HEAD_DIM), jnp.float32)

        def gather_blocks(idx):
            phys = pt[idx[..., None] * ppb + jnp.arange(ppb)][..., None]
            shp = (*idx.shape, MOBA_BLOCK, HEAD_DIM)
            return cache_k[l, phys, off, hsel].reshape(shp), cache_v[l, phys, off, hsel].reshape(shp)

        own_pages = pt[nb_full * ppb: nb_full * ppb + r_pages]
        k_own = jnp.concatenate([cache_k[l, own_pages].reshape(r_rows, A_HEADS, HEAD_DIM), kn], axis=0)
        v_own = jnp.concatenate([cache_v[l, own_pages].reshape(r_rows, A_HEADS, HEAD_DIM), vn], axis=0)
        return _moba_attend(q, tpos, means, gather_blocks, jnp.moveaxis(k_own, 0, 1),
                            jnp.moveaxis(v_own, 0, 1), own_pos, slopes)

    return lax.map(one, (qa, ka, va, page_table))


def _dsa_attend(q, iq, iw, ik_all, tpos, gather_rows, slopes, k_sel):
    rel = jax.nn.relu(jnp.einsum('qjd,sd->qjs', iq.astype(jnp.float32), ik_all.astype(jnp.float32)) * IDX_SCALE)
    score = jnp.einsum('qjs,qj->qs', rel, iw.astype(jnp.float32) * IDX_W_SCALE)
    kpos = jnp.arange(ik_all.shape[0])
    score = jnp.where(kpos[None, :] <= tpos[:, None], score, -jnp.inf)
    _, idx = lax.top_k(score, k_sel)
    k_g, v_g = gather_rows(idx)
    dist = tpos[:, None, None] - idx[:, None, :]
    s = _biased(jnp.einsum('qhd,qkhd->qhk', q, k_g), dist, slopes[None, :, None], dist >= 0)
    p = jax.nn.softmax(s, axis=-1)
    return jnp.einsum('qhk,qkhd->qhd', p, v_g.astype(jnp.float32))


def _dsa_prompt(qb, kb, vb, iq, ik, iw, slopes):
    bsz, seq = qb.shape[:2]
    k_sel = min(DSA_TOPK, seq // 4)
    n_qb = seq // Q_BLOCK

    def one(i):
        b = i // n_qb
        t0 = (i % n_qb) * Q_BLOCK
        tpos = t0 + jnp.arange(Q_BLOCK)
        q = lax.dynamic_slice(qb, (b, t0, 0, 0), (1, Q_BLOCK, B_HEADS, HEAD_DIM))[0]
        iq_t = lax.dynamic_slice(iq, (b, t0, 0, 0), (1, Q_BLOCK, IDX_HEADS, IDX_DIM))[0]
        iw_t = lax.dynamic_slice(iw, (b, t0, 0), (1, Q_BLOCK, IDX_HEADS))[0]

        def gather_rows(idx):
            return kb[b, idx], vb[b, idx]

        return _dsa_attend(q, iq_t, iw_t, ik[b], tpos, gather_rows, slopes, k_sel)

    out = lax.map(one, jnp.arange(bsz * n_qb))
    return out.reshape(bsz, seq, B_HEADS, HEAD_DIM)


def _dsa_sample(qb, kb, vb, iq, ik, iw, cache_k, cache_v, cache_ik, page_table, l, slopes):
    n_tok = qb.shape[1]
    k_sel = min(DSA_TOPK, (PAST_LEN + n_tok) // 4)
    tpos = PAST_LEN + jnp.arange(n_tok)

    def one(args):
        q, kn, vn, iq_t, ik_n, iw_t, pt = args
        ik_all = jnp.concatenate([cache_ik[l, pt].reshape(PAST_LEN, IDX_DIM), ik_n], axis=0)

        def gather_rows(idx):
            is_new = (idx >= PAST_LEN)[..., None, None]
            pidx = jnp.minimum(idx, PAST_LEN - 1)
            phys = pt[pidx // PAGE_SIZE]
            poff = pidx % PAGE_SIZE
            nidx = jnp.clip(idx - PAST_LEN, 0, n_tok - 1)
            k = jnp.where(is_new, kn[nidx], cache_k[l, phys, poff])
            v = jnp.where(is_new, vn[nidx], cache_v[l, phys, poff])
            return k, v

        return _dsa_attend(q, iq_t, iw_t, ik_all, tpos, gather_rows, slopes, k_sel)

    return lax.map(one, (qb, kb, vb, iq, ik, iw, page_table))


def _tail(x, o_a, o_b, o_m, w_o, g1, b1, w1, w2, g2, b2):
    lead = x.shape[:-1]
    o = jnp.concatenate([o_a.reshape(*lead, A_W), o_b.reshape(*lead, B_W),
                         o_m.reshape(*lead, M_W)], axis=-1).astype(x.dtype)
    x = _layer_norm(ALPHA * x + o @ w_o, g1, b1)
    h = jnp.square(jax.nn.relu(x @ w1))
    return _layer_norm(ALPHA * x + h @ w2, g2, b2)


def setup_inputs(seed: int = 0) -> dict:
    key = jax.random.key(seed)
    ks = jax.random.split(key, 24)
    f32 = jnp.float32
    n_pages = PAST_LEN // PAGE_SIZE
    n_used = DEC_BATCH * n_pages
    n_pool = n_used + max(1, n_used // 4)

    def nrm(k, shape, s=1.0):
        return jax.random.normal(k, shape, f32) * s

    col_scale = jnp.concatenate([jnp.full((w,), BETA if i in V_SLOTS else 1.0, f32)
                                 for i, w in enumerate(SPLITS)])
    mem_scale = jnp.concatenate([jnp.ones((M_W,), f32), jnp.full((M_W,), BETA, f32)])
    page_table = jax.random.permutation(ks[0], n_pool)[:n_used].reshape(DEC_BATCH, n_pages).astype(jnp.int32)
    return {
        'x_prompt': nrm(ks[1], (BATCH, SEQ, D_MODEL)),
        'x_sample': nrm(ks[2], (DEC_BATCH, DEC_SEQ, D_MODEL)),
        'cache_a_k': nrm(ks[3], (DEPTH, n_pool, PAGE_SIZE, A_HEADS, HEAD_DIM)),
        'cache_a_v': nrm(ks[4], (DEPTH, n_pool, PAGE_SIZE, A_HEADS, HEAD_DIM)),
        'cache_b_k': nrm(ks[5], (DEPTH, n_pool, PAGE_SIZE, B_HEADS, HEAD_DIM)),
        'cache_b_v': nrm(ks[6], (DEPTH, n_pool, PAGE_SIZE, B_HEADS, HEAD_DIM)),
        'cache_idx_k': nrm(ks[7], (DEPTH, n_pool, PAGE_SIZE, IDX_DIM)),
        'cache_mem_k': nrm(ks[8], (DEPTH, DEC_BATCH, N_MEM, M_HEADS, HEAD_DIM)),
        'cache_mem_v': nrm(ks[9], (DEPTH, DEC_BATCH, N_MEM, M_HEADS, HEAD_DIM)),
        'page_table': page_table,
        'mem_prompt': nrm(ks[10], (BATCH, N_MEM, D_MODEL)),
        'w_in': nrm(ks[11], (DEPTH, D_MODEL, PROJ_W), D_MODEL ** -0.5) * col_scale,
        'w_mem_kv': nrm(ks[12], (DEPTH, D_MODEL, 2 * M_W), D_MODEL ** -0.5) * mem_scale,
        'w_out': nrm(ks[13], (DEPTH, D_MODEL, D_MODEL), BETA * D_MODEL ** -0.5),
        'ln1_g': 1.0 + nrm(ks[14], (DEPTH, D_MODEL), 0.02),
        'ln1_b': nrm(ks[15], (DEPTH, D_MODEL), 0.02),
        'w_ff1': nrm(ks[16], (DEPTH, D_MODEL, D_FF), BETA * D_MODEL ** -0.5),
        'w_ff2': nrm(ks[17], (DEPTH, D_FF, D_MODEL), BETA * D_FF ** -0.5),
        'ln2_g': 1.0 + nrm(ks[18], (DEPTH, D_MODEL), 0.02),
        'ln2_b': nrm(ks[19], (DEPTH, D_MODEL), 0.02),
    }


def reference(x_prompt, x_sample, cache_a_k, cache_a_v, cache_b_k, cache_b_v, cache_idx_k,
              cache_mem_k, cache_mem_v, page_table, mem_prompt, w_in, w_mem_kv, w_out,
              ln1_g, ln1_b, w_ff1, w_ff2, ln2_g, ln2_b):
    slopes_a = _alibi_slopes(A_HEADS)
    slopes_b = _alibi_slopes(B_HEADS)

    x = x_prompt
    p_states = []
    for l in range(DEPTH):
        qa, ka, va, qb, kb, vb, iq, ik, iw, qm = _project(x, w_in[l])
        km, vm = _mem_kv(mem_prompt, w_mem_kv[l])
        o_a = _moba_prompt(qa, ka, va, slopes_a)
        o_b = _dsa_prompt(qb, kb, vb, iq, ik, iw, slopes_b)
        o_m = _mem_attend(qm, km, vm)
        x = _tail(x, o_a, o_b, o_m, w_out[l], ln1_g[l], ln1_b[l], w_ff1[l], w_ff2[l], ln2_g[l], ln2_b[l])
        p_states.append((ka, va, kb, vb, ik, km, vm))
    y_prompt = x

    x = x_sample
    s_states = []
    for l in range(DEPTH):
        qa, ka, va, qb, kb, vb, iq, ik, iw, qm = _project(x, w_in[l])
        o_a = _moba_sample(qa, ka, va, cache_a_k, cache_a_v, page_table, l, slopes_a)
        o_b = _dsa_sample(qb, kb, vb, iq, ik, iw, cache_b_k, cache_b_v, cache_idx_k, page_table, l, slopes_b)
        o_m = _mem_attend(qm, cache_mem_k[l], cache_mem_v[l])
        x = _tail(x, o_a, o_b, o_m, w_out[l], ln1_g[l], ln1_b[l], w_ff1[l], w_ff2[l], ln2_g[l], ln2_b[l])
        s_states.append((ka, va, kb, vb, ik))
    y_sample = x

    new_a_k_p, new_a_v_p, new_b_k_p, new_b_v_p, new_idx_k_p, new_mem_k_p, new_mem_v_p = [
        jnp.stack(z, axis=0) for z in zip(*p_states)]
    new_a_k_s, new_a_v_s, new_b_k_s, new_b_v_s, new_idx_k_s = [
        jnp.stack(z, axis=0) for z in zip(*s_states)]
    return (y_prompt, y_sample, new_a_k_p, new_a_v_p, new_b_k_p, new_b_v_p, new_idx_k_p,
            new_mem_k_p, new_mem_v_p, new_a_k_s, new_a_v_s, new_b_k_s, new_b_v_s, new_idx_k_s)
```

```python
import functools

import jax
import jax.numpy as jnp
from jax import lax
from jax.experimental import pallas as pl
from jax.experimental.pallas import tpu as pltpu

F32, BF16, I32 = jnp.float32, jnp.bfloat16, jnp.int32

HEAD_DIM = 64
A_HEADS, B_HEADS, M_HEADS = 8, 4, 4
IDX_HEADS, IDX_DIM = 8, 32
A_W, B_W, M_W = A_HEADS * HEAD_DIM, B_HEADS * HEAD_DIM, M_HEADS * HEAD_DIM
IDX_W = IDX_HEADS * IDX_DIM
MOBA_BLOCK, MOBA_TOPK, DSA_TOPK, PAGE_SIZE = 256, 3, 256, 128
LN_EPS = 1e-5
ATTN_SCALE = HEAD_DIM ** -0.5
IDX_SCALE = IDX_DIM ** -0.5
IDX_W_SCALE = IDX_HEADS ** -0.5
_SPLITS = (A_W, A_W, A_W, B_W, B_W, B_W, IDX_W, IDX_DIM, IDX_HEADS, M_W)
_OFF = tuple(sum(_SPLITS[:i]) for i in range(len(_SPLITS) + 1))

LANES = 128
KEY_BLOCK = 256
ROW_CHUNK = 64
NEG_INF = float("-inf")
M_FLOOR = -1e30
VMEM_LIMIT = 56 * 1024 * 1024
SAMPLE_PAGES_PER_STEP = 16


def _mm(a, b):
    return jnp.dot(a, b, preferred_element_type=F32)


def _nt(a, b):
    return lax.dot_general(a, b, (((1,), (1,)), ((), ())), preferred_element_type=F32)


def _params(n_grid_dims):
    return pltpu.CompilerParams(dimension_semantics=("arbitrary",) * n_grid_dims,
                                vmem_limit_bytes=VMEM_LIMIT)


def _smem():
    return pl.BlockSpec(memory_space=pltpu.SMEM)


def _slopes3(sl_ref, n):
    i = lax.broadcasted_iota(I32, (n, 1, 1), 0)
    out = jnp.full((n, 1, 1), sl_ref[0], F32)
    for h in range(1, n):
        out = jnp.where(i == h, sl_ref[h], out)
    return out


def _stack_heads(q2, n_heads, width):
    lane = lax.broadcasted_iota(I32, q2.shape, 1)
    zero = jnp.zeros_like(q2)
    return jnp.concatenate(
        [jnp.where((lane >= h * width) & (lane < (h + 1) * width), q2, zero) for h in range(n_heads)], axis=0)


def _merge_heads(x, n_heads, t, width):
    lane = lax.broadcasted_iota(I32, (t, n_heads * width), 1)
    out = x[0:t]
    for h in range(1, n_heads):
        out = jnp.where(lane >= h * width, x[h * t:(h + 1) * t], out)
    return out


def _top_select(gate, n_sel):
    blk = lax.broadcasted_iota(I32, gate.shape, 1)
    sel = jnp.zeros(gate.shape, F32)
    g = gate
    for _ in range(n_sel):
        m = jnp.max(g, axis=1, keepdims=True)
        first = jnp.min(jnp.where(g == m, blk, jnp.int32(1 << 30)), axis=1, keepdims=True)
        pick = (blk == first) & (m > NEG_INF)
        sel = jnp.where(pick, 1.0, sel)
        g = jnp.where(pick, NEG_INF, g)
    return sel


def _key_to_f32(key):
    bits = jnp.where(key < 0, key ^ jnp.int32(0x7FFFFFFF), key)
    return lax.bitcast_convert_type(bits, F32)


def _kth_largest(count, k, rows, idx_bits):
    kf = jnp.float32(k)
    int_min = jnp.int32(-2 ** 31)
    ge0 = count(lambda v, i: v >= 0.0)
    key = jnp.where(ge0 >= kf, jnp.zeros((rows, 1), I32), jnp.full((rows, 1), int_min, I32))

    def vbody(b, key):
        cand = key | lax.shift_left(jnp.int32(1), jnp.int32(30) - b)
        cf = _key_to_f32(cand)
        return jnp.where(count(lambda v, i: v >= cf) >= kf, cand, key)

    key = lax.fori_loop(0, 31, vbody, key)
    thr = _key_to_f32(key)
    thr = jnp.where(thr != thr, NEG_INF, thr)
    need = kf - count(lambda v, i: v > thr)

    def ibody(b, cut):
        cand = cut | lax.shift_left(jnp.int32(1), jnp.int32(idx_bits - 1) - b)
        below = count(lambda v, i: (v == thr) & (i < cand))
        return jnp.where(below < need, cand, cut)

    cut = lax.fori_loop(0, idx_bits, ibody, jnp.zeros((rows, 1), I32))
    return thr, cut


def _layer_norm(x, g, b):
    mu = jnp.mean(x, axis=-1, keepdims=True)
    xc = x - mu
    var = jnp.mean(xc * xc, axis=-1, keepdims=True)
    return xc * lax.rsqrt(var + LN_EPS) * g + b


_NAT_W = A_W + B_W + IDX_W + M_W + LANES
_TR_W = 2 * A_W + 2 * B_W + IDX_DIM


def _split_w_in(w):
    o = _OFF
    pad = jnp.zeros((w.shape[0], LANES - IDX_HEADS), w.dtype)
    wn = jnp.concatenate([w[:, o[0]:o[1]], w[:, o[3]:o[4]], w[:, o[6]:o[7]], w[:, o[9]:o[10]],
                          w[:, o[8]:o[9]], pad], axis=1)
    wt = jnp.concatenate([w[:, o[1]:o[2]], w[:, o[2]:o[3]], w[:, o[4]:o[5]], w[:, o[5]:o[6]],
                          w[:, o[7]:o[8]]], axis=1).T
    return wn.astype(BF16), wt.astype(BF16)


def _proj_prompt_kernel(x_ref, wn_ref, wt_ref, qa_ref, qb_ref, iq_ref, qm_ref, iw_ref,
                        kaT_ref, vaT_ref, kbT_ref, vbT_ref, ikT_ref):
    xb = x_ref[0].astype(BF16)
    nat = _mm(xb, wn_ref[...])
    c = 0
    for ref, w in ((qa_ref, A_W), (qb_ref, B_W), (iq_ref, IDX_W), (qm_ref, M_W)):
        ref[0] = nat[:, c:c + w].astype(BF16)
        c += w
    iw_ref[0] = nat[:, c:c + LANES]
    tr = _nt(wt_ref[...], xb)
    c = 0
    for ref, w in ((kaT_ref, A_W), (vaT_ref, A_W), (kbT_ref, B_W), (vbT_ref, B_W), (ikT_ref, IDX_DIM)):
        ref[0] = tr[c:c + w]
        c += w


def _proj_prompt(x, wn, wt, tm):
    bsz, seq, d = x.shape
    nat = lambda w: pl.BlockSpec((1, tm, w), lambda b, i: (b, i, 0))
    tr = lambda w: pl.BlockSpec((1, w, tm), lambda b, i: (b, 0, i))
    full = lambda a: pl.BlockSpec(a.shape, lambda b, i: (0, 0))
    return pl.pallas_call(
        _proj_prompt_kernel,
        grid=(bsz, seq // tm),
        in_specs=[nat(d), full(wn), full(wt)],
        out_specs=[nat(A_W), nat(B_W), nat(IDX_W), nat(M_W), nat(LANES),
                   tr(A_W), tr(A_W), tr(B_W), tr(B_W), tr(IDX_DIM)],
        out_shape=[jax.ShapeDtypeStruct((bsz, seq, w), BF16) for w in (A_W, B_W, IDX_W, M_W)]
        + [jax.ShapeDtypeStruct((bsz, seq, LANES), F32)]
        + [jax.ShapeDtypeStruct((bsz, w, seq), F32) for w in (A_W, A_W, B_W, B_W, IDX_DIM)],
        compiler_params=_params(2),
        name="proj_prompt",
    )(x, wn, wt)


def _matmul_kernel(x_ref, w_ref, o_ref):
    o_ref[...] = _mm(x_ref[...].astype(BF16), w_ref[...])


def _matmul(x, w):
    m, n = x.shape[0], w.shape[1]
    return pl.pallas_call(
        _matmul_kernel,
        grid=(1,),
        in_specs=[pl.BlockSpec(x.shape, lambda i: (0, 0)), pl.BlockSpec(w.shape, lambda i: (0, 0))],
        out_specs=pl.BlockSpec((m, n), lambda i: (0, 0)),
        out_shape=jax.ShapeDtypeStruct((m, n), F32),
        compiler_params=_params(1),
        name="proj_sample",
    )(x, w)


def _mem_kv_kernel(wT_ref, m_ref, o_ref):
    o_ref[0] = _nt(wT_ref[...], m_ref[0].astype(BF16))


def _mem_kv(mem, wT):
    bsz, n_mem, d = mem.shape
    return pl.pallas_call(
        _mem_kv_kernel,
        grid=(bsz,),
        in_specs=[pl.BlockSpec(wT.shape, lambda b: (0, 0)), pl.BlockSpec((1, n_mem, d), lambda b: (b, 0, 0))],
        out_specs=pl.BlockSpec((1, wT.shape[0], n_mem), lambda b: (b, 0, 0)),
        out_shape=jax.ShapeDtypeStruct((bsz, wT.shape[0], n_mem), F32),
        compiler_params=_params(1),
        name="mem_kv",
    )(wT, mem)


def _moba_prompt_kernel(sl_ref, q_ref, kT_ref, vT_ref, o_ref, ks, vs, mT, *, tq, nb, n_sel):
    hp = pl.program_id(1)
    qi = pl.program_id(2)
    pair_w = 2 * HEAD_DIM

    @pl.when(qi == 0)
    def _():
        lane = lax.broadcasted_iota(I32, (pair_w, LANES), 1)
        means = jnp.zeros((pair_w, LANES), F32)
        for j in range(nb):
            kblk = kT_ref[0, :, j * KEY_BLOCK:(j + 1) * KEY_BLOCK]
            ks[j] = kblk.astype(BF16)
            vs[j] = vT_ref[0, :, j * KEY_BLOCK:(j + 1) * KEY_BLOCK].astype(BF16)
            means = jnp.where(lane == j, jnp.mean(kblk, axis=1, keepdims=True), means)
        mT[...] = means.astype(BF16)

    m_rows = 2 * tq
    qs = _stack_heads(q_ref[0], 2, HEAD_DIM)
    n_past = qi
    gate = _mm(qs, mT[...])
    blk = lax.broadcasted_iota(I32, (m_rows, LANES), 1)
    gate = jnp.where(blk < n_past, gate, NEG_INF)
    sel = _top_select(gate, n_sel).astype(BF16)

    row = lax.broadcasted_iota(I32, (m_rows, KEY_BLOCK), 0)
    lane = lax.broadcasted_iota(I32, (m_rows, KEY_BLOCK), 1)
    rel = jnp.where(row >= tq, row - tq, row) - lane
    rowc = lax.broadcasted_iota(I32, (m_rows, 1), 0)
    slope = jnp.where(rowc >= tq, sl_ref[2 * hp + 1], sl_ref[2 * hp])

    def scores(j):
        dist = rel + (qi - j) * KEY_BLOCK
        s = _mm(qs, ks[j]) * ATTN_SCALE - slope * dist.astype(F32)
        return s, dist

    s, dist = scores(qi)
    s = jnp.where(dist >= 0, s, NEG_INF)
    m = jnp.max(s, axis=1, keepdims=True)
    p = jnp.exp(s - m)
    l = jnp.sum(p, axis=1, keepdims=True)
    acc = _nt(p.astype(BF16), vs[qi])

    def body(j, carry):
        m, l, acc = carry
        s, _ = scores(j)
        onehot = jnp.where(lax.broadcasted_iota(I32, (LANES, KEY_BLOCK), 0) == j, 1.0, 0.0).astype(BF16)
        chosen = _mm(sel, onehot)
        s = jnp.where(chosen > 0.5, s, NEG_INF)
        m_new = jnp.maximum(m, jnp.max(s, axis=1, keepdims=True))
        a = jnp.exp(m - m_new)
        p = jnp.exp(s - m_new)
        l = a * l + jnp.sum(p, axis=1, keepdims=True)
        acc = a * acc + _nt(p.astype(BF16), vs[j])
        return m_new, l, acc

    m, l, acc = lax.fori_loop(0, n_past, body, (m, l, acc))
    o_ref[0] = _merge_heads(acc / l, 2, tq, HEAD_DIM).astype(o_ref.dtype)


def _moba_prompt(qa, kaT, vaT, slopes):
    bsz, seq, _ = qa.shape
    tq = MOBA_BLOCK
    nb = seq // KEY_BLOCK
    pair_w = 2 * HEAD_DIM
    kern = functools.partial(_moba_prompt_kernel, tq=tq, nb=nb, n_sel=min(MOBA_TOPK, nb))
    kv = pl.BlockSpec((1, pair_w, seq), lambda b, hp, qi: (b, hp, 0))
    return pl.pallas_call(
        kern,
        grid=(bsz, A_HEADS // 2, seq // tq),
        in_specs=[_smem(), pl.BlockSpec((1, tq, pair_w), lambda b, hp, qi: (b, qi, hp)), kv, kv],
        out_specs=pl.BlockSpec((1, tq, pair_w), lambda b, hp, qi: (b, qi, hp)),
        out_shape=jax.ShapeDtypeStruct((bsz, seq, A_W), BF16),
        scratch_shapes=[pltpu.VMEM((nb, pair_w, KEY_BLOCK), BF16), pltpu.VMEM((nb, pair_w, KEY_BLOCK), BF16),
                        pltpu.VMEM((pair_w, LANES), BF16)],
        compiler_params=_params(3),
        name="moba_prompt",
    )(slopes, qa, kaT, vaT)


def _dsa_prompt_kernel(sl_ref, q_ref, iq_ref, iw_ref, ikT_ref, kT_ref, vT_ref, o_ref,
                       ikr, kbs, vbs, keys, iqm, wb, thr_ref, cut_ref, *, tq, nb, k_sel, idx_bits):
    qi = pl.program_id(1)

    @pl.when(qi == 0)
    def _():
        for j in range(nb):
            cols = slice(j * KEY_BLOCK, (j + 1) * KEY_BLOCK)
            ik = ikT_ref[0, :, cols].astype(BF16)
            ikr[j] = jnp.concatenate([ik] * IDX_HEADS, axis=0)
            kbs[j] = kT_ref[0, :, cols].astype(BF16)
            vbs[j] = vT_ref[0, :, cols].astype(BF16)

    t0 = qi * tq
    nkb = qi + 1
    iqm[...] = _stack_heads(iq_ref[0], IDX_HEADS, IDX_DIM)
    w = iw_ref[0] * (IDX_W_SCALE * IDX_SCALE)
    for h in range(IDX_HEADS):
        wb[h] = jnp.broadcast_to(w[:, h:h + 1], (tq, KEY_BLOCK))

    row = lax.broadcasted_iota(I32, (tq, KEY_BLOCK), 0)
    lane = lax.broadcasted_iota(I32, (tq, KEY_BLOCK), 1)
    tpos = t0 + row

    def idx_body(jb, carry):
        r = _mm(iqm[...], ikr[jb])
        sc = jnp.zeros((tq, KEY_BLOCK), F32)
        for h in range(IDX_HEADS):
            sc = sc + jnp.maximum(r[h * tq:(h + 1) * tq], 0.0) * wb[h]
        keys[jb] = jnp.where(jb * KEY_BLOCK + lane <= tpos, sc, NEG_INF)
        return carry

    lax.fori_loop(0, nkb, idx_body, 0)

    lane_c = lax.broadcasted_iota(I32, (ROW_CHUNK, KEY_BLOCK), 1)
    for rc in range(tq // ROW_CHUNK):
        rows = pl.ds(rc * ROW_CHUNK, ROW_CHUNK)

        def count(pred, rows=rows):
            def body(jb, c):
                hit = jnp.where(pred(keys[jb, rows, :], jb * KEY_BLOCK + lane_c), 1.0, 0.0)
                return c + hit[:, :LANES] + hit[:, LANES:]
            c = lax.fori_loop(0, nkb, body, jnp.zeros((ROW_CHUNK, LANES), F32))
            return jnp.sum(c, axis=1, keepdims=True)

        thr, cut = _kth_largest(count, k_sel, ROW_CHUNK, idx_bits)
        thr_ref[rows, :] = thr
        cut_ref[rows, :] = cut

    thr = thr_ref[...]
    cut = cut_ref[...]
    qs = _stack_heads(q_ref[0], B_HEADS, HEAD_DIM)
    slope3 = _slopes3(sl_ref, B_HEADS)
    m_rows = B_HEADS * tq

    def att_body(jb, carry):
        m, l, acc = carry
        kk = keys[jb]
        idx = jb * KEY_BLOCK + lane
        chosen = ((kk > thr) | ((kk == thr) & (idx <= cut))) & (idx <= tpos)
        dist = (tpos - idx).astype(F32)
        s3 = _mm(qs, kbs[jb]).reshape(B_HEADS, tq, KEY_BLOCK) * ATTN_SCALE - slope3 * dist[None]
        s = jnp.where(chosen[None], s3, NEG_INF).reshape(m_rows, KEY_BLOCK)
        m_new = jnp.maximum(m, jnp.max(s, axis=1, keepdims=True))
        a = jnp.exp(m - m_new)
        p = jnp.exp(s - m_new)
        l = a * l + jnp.sum(p, axis=1, keepdims=True)
        acc = a * acc + _nt(p.astype(BF16), vbs[jb])
        return m_new, l, acc

    init = (jnp.full((m_rows, 1), M_FLOOR, F32), jnp.zeros((m_rows, 1), F32), jnp.zeros((m_rows, B_W), F32))
    m, l, acc = lax.fori_loop(0, nkb, att_body, init)
    o_ref[0] = _merge_heads(acc / l, B_HEADS, tq, HEAD_DIM).astype(o_ref.dtype)


def _dsa_prompt(qb, iq, iw, ikT, kbT, vbT, slopes):
    bsz, seq, _ = qb.shape
    tq = KEY_BLOCK
    nb = seq // KEY_BLOCK
    k_sel = min(DSA_TOPK, seq // 4)
    kern = functools.partial(_dsa_prompt_kernel, tq=tq, nb=nb, k_sel=k_sel, idx_bits=(seq - 1).bit_length())
    qspec = lambda w: pl.BlockSpec((1, tq, w), lambda b, qi: (b, qi, 0))
    kspec = lambda w: pl.BlockSpec((1, w, seq), lambda b, qi: (b, 0, 0))
    return pl.pallas_call(
        kern,
        grid=(bsz, seq // tq),
        in_specs=[_smem(), qspec(B_W), qspec(IDX_W), qspec(LANES), kspec(IDX_DIM), kspec(B_W), kspec(B_W)],
        out_specs=qspec(B_W),
        out_shape=jax.ShapeDtypeStruct((bsz, seq, B_W), BF16),
        scratch_shapes=[pltpu.VMEM((nb, IDX_W, KEY_BLOCK), BF16), pltpu.VMEM((nb, B_W, KEY_BLOCK), BF16),
                        pltpu.VMEM((nb, B_W, KEY_BLOCK), BF16), pltpu.VMEM((nb, tq, KEY_BLOCK), F32),
                        pltpu.VMEM((IDX_HEADS * tq, IDX_W), BF16), pltpu.VMEM((IDX_HEADS, tq, KEY_BLOCK), F32),
                        pltpu.VMEM((tq, 1), F32), pltpu.VMEM((tq, 1), I32)],
        compiler_params=_params(2),
        name="dsa_prompt",
    )(slopes, qb, iq, iw, ikT, kbT, vbT)


def _mem_attn_kernel(q_ref, kT_ref, vT_ref, o_ref, *, tq):
    qs = _stack_heads(q_ref[0].astype(F32), M_HEADS, HEAD_DIM).astype(BF16)
    s = _mm(qs, kT_ref[0].astype(BF16)) * ATTN_SCALE
    p = jnp.exp(s - jnp.max(s, axis=1, keepdims=True))
    l = jnp.sum(p, axis=1, keepdims=True)
    o = _nt(p.astype(BF16), vT_ref[0].astype(BF16)) / l
    o_ref[0] = _merge_heads(o, M_HEADS, tq, HEAD_DIM).astype(o_ref.dtype)


def _mem_attn(qm, kT, vT, tq):
    g, t, _ = qm.shape
    n_mem = kT.shape[2]
    kv = pl.BlockSpec((1, M_W, n_mem), lambda b, i: (b, 0, 0))
    qspec = pl.BlockSpec((1, tq, M_W), lambda b, i: (b, i, 0))
    return pl.pallas_call(
        functools.partial(_mem_attn_kernel, tq=tq),
        grid=(g, t // tq),
        in_specs=[qspec, kv, kv],
        out_specs=qspec,
        out_shape=jax.ShapeDtypeStruct((g, t, M_W), BF16),
        compiler_params=_params(2),
        name="mem_attn",
    )(qm, kT, vT)


def _tail_kernel(x_ref, oa_ref, ob_ref, om_ref, wo_ref, g1_ref, b1_ref, w1_ref, w2_ref, g2_ref, b2_ref,
                 y_ref, *, alpha, ff_chunk):
    attn = (_mm(oa_ref[...], wo_ref[0:A_W, :]) + _mm(ob_ref[...], wo_ref[A_W:A_W + B_W, :])
            + _mm(om_ref[...], wo_ref[A_W + B_W:A_W + B_W + M_W, :]))
    x1 = _layer_norm(alpha * x_ref[...] + attn, g1_ref[...], b1_ref[...])
    x1b = x1.astype(BF16)
    d_ff = w1_ref.shape[1]
    acc = jnp.zeros(x1.shape, F32)
    for c in range(0, d_ff, ff_chunk):
        h = jnp.square(jnp.maximum(_mm(x1b, w1_ref[:, c:c + ff_chunk]), 0.0))
        acc = acc + _mm(h.astype(BF16), w2_ref[c:c + ff_chunk, :])
    y_ref[...] = _layer_norm(alpha * x1 + acc, g2_ref[...], b2_ref[...])


def _tail(x, oa, ob, om, wo, g1, b1, w1, w2, g2, b2, alpha, tm):
    n, d = x.shape
    rows = lambda w: pl.BlockSpec((tm, w), lambda i: (i, 0))
    const = lambda a: pl.BlockSpec(a.shape, lambda i: (0, 0), pipeline_mode=pl.Buffered(1))
    return pl.pallas_call(
        functools.partial(_tail_kernel, alpha=alpha, ff_chunk=min(1024, w1.shape[1])),
        grid=(n // tm,),
        in_specs=[rows(d), rows(A_W), rows(B_W), rows(M_W), const(wo), const(g1), const(b1), const(w1),
                  const(w2), const(g2), const(b2)],
        out_specs=rows(d),
        out_shape=jax.ShapeDtypeStruct((n, d), F32),
        compiler_params=_params(1),
        name="tail",
    )(x, oa, ob, om, wo, g1, b1, w1, w2, g2, b2)


def _page_specs(n, width, layer, pp, n_groups, phase_of_use):
    def index(i):
        def f(r, ph, g, pt):
            grp = jnp.where(ph == 0, g, n_groups - 1) if phase_of_use == 0 else jnp.where(ph == 0, 0, g)
            return (layer, pt[r, grp * pp + i], 0, 0)
        return f
    return [pl.BlockSpec((None, None, width, PAGE_SIZE), index(i)) for i in range(pp)]


def _moba_sample_kernel(pt_ref, sl_ref, q_ref, knT_ref, vnT_ref, *rest, pp, n_groups, n_pages, n_tok, past, n_sel):
    k_refs, v_refs = rest[:pp], rest[pp:2 * pp]
    o_ref, sc, acc_ref, l_ref = rest[2 * pp:]
    ph = pl.program_id(1)
    g = pl.program_id(2)
    m_rows = A_HEADS * n_tok
    q = q_ref[0]

    @pl.when(ph == 0)
    def _():
        for i in range(pp):
            sc[g * pp + i] = _mm(q, k_refs[i][...].astype(BF16))

    @pl.when((ph == 0) & (g == n_groups - 1))
    def _():
        sc[n_pages] = _mm(q, knT_ref[0].astype(BF16))
        lane = lax.broadcasted_iota(I32, (m_rows, LANES), 1)
        ppb = MOBA_BLOCK // PAGE_SIZE
        n_blocks = n_pages // ppb

        def gate_body(n, gate):
            tot = sc[ppb * n]
            for u in range(1, ppb):
                tot = tot + sc[ppb * n + u]
            return jnp.where(lane == n, jnp.sum(tot, axis=1, keepdims=True) * (1.0 / MOBA_BLOCK), gate)

        gate = lax.fori_loop(0, n_blocks, gate_body, jnp.full((m_rows, LANES), NEG_INF, F32))
        sel = _top_select(gate, n_sel)

        slope3 = _slopes3(sl_ref, A_HEADS)
        lane_t = lax.broadcasted_iota(I32, (n_tok, LANES), 1)
        tok = lax.broadcasted_iota(I32, (n_tok, LANES), 0)

        def bias_body(p, m):
            chosen = jnp.sum(jnp.where(lane == p // ppb, sel, 0.0), axis=1, keepdims=True) > 0.5
            dist = (past + tok - (p * PAGE_SIZE + lane_t)).astype(F32)
            s3 = sc[p].reshape(A_HEADS, n_tok, LANES) * ATTN_SCALE - slope3 * dist[None]
            s = jnp.where(chosen, s3.reshape(m_rows, LANES), NEG_INF)
            sc[p] = s
            return jnp.maximum(m, jnp.max(s, axis=1, keepdims=True))

        m = lax.fori_loop(0, n_pages, bias_body, jnp.full((m_rows, 1), M_FLOOR, F32))
        dist = tok - lane_t
        s3 = sc[n_pages].reshape(A_HEADS, n_tok, LANES) * ATTN_SCALE - slope3 * dist.astype(F32)[None]
        s = jnp.where((dist >= 0)[None], s3, NEG_INF).reshape(m_rows, LANES)
        sc[n_pages] = s
        m = jnp.maximum(m, jnp.max(s, axis=1, keepdims=True))

        def exp_body(p, l):
            e = jnp.exp(sc[p] - m)
            sc[p] = e
            return l + jnp.sum(e, axis=1, keepdims=True)

        l_ref[...] = lax.fori_loop(0, n_pages + 1, exp_body, jnp.zeros((m_rows, 1), F32))

    @pl.when(ph == 1)
    def _():
        @pl.when(g == 0)
        def _():
            acc_ref[...] = jnp.zeros_like(acc_ref)
        a = acc_ref[...]
        for i in range(pp):
            a = a + _nt(sc[g * pp + i].astype(BF16), v_refs[i][...].astype(BF16))
        acc_ref[...] = a

    @pl.when((ph == 1) & (g == n_groups - 1))
    def _():
        a = acc_ref[...] + _nt(sc[n_pages].astype(BF16), vnT_ref[0].astype(BF16))
        o_ref[0] = _merge_heads(a / l_ref[...], A_HEADS, n_tok, HEAD_DIM).astype(o_ref.dtype)


def _block_diag_rows(q):
    r, t, h, dh = q.shape
    qh = jnp.transpose(q, (0, 2, 1, 3))
    eye = jnp.eye(h, dtype=bool)[None, :, None, :, None]
    out = jnp.where(eye, qh[:, :, :, None, :], 0.0)
    return out.reshape(r, h * t, h * dh).astype(BF16)


def _new_T(x):
    xt = jnp.transpose(x, (0, 2, 1))
    return jnp.pad(xt, ((0, 0), (0, 0), (0, PAGE_SIZE - x.shape[1])))


def _moba_sample(qa, ka, va, cache_kT, cache_vT, page_table, layer, slopes):
    r, t, _ = qa.shape
    n_pages = page_table.shape[1]
    pp = min(SAMPLE_PAGES_PER_STEP, n_pages)
    n_groups = n_pages // pp
    m_rows = A_HEADS * t
    n_blocks = n_pages * PAGE_SIZE // MOBA_BLOCK
    kern = functools.partial(_moba_sample_kernel, pp=pp, n_groups=n_groups, n_pages=n_pages, n_tok=t,
                             past=n_pages * PAGE_SIZE, n_sel=min(MOBA_TOPK, n_blocks))
    per_req = lambda a: pl.BlockSpec((1,) + a.shape[1:], lambda rr, ph, g, pt: (rr, 0, 0))
    qbd = _block_diag_rows(qa.reshape(r, t, A_HEADS, HEAD_DIM))
    knT, vnT = _new_T(ka), _new_T(va)
    grid_spec = pltpu.PrefetchScalarGridSpec(
        num_scalar_prefetch=1,
        grid=(r, 2, n_groups),
        in_specs=[_smem(), per_req(qbd), per_req(knT), per_req(vnT)]
        + _page_specs(n_pages, A_W, layer, pp, n_groups, 0) + _page_specs(n_pages, A_W, layer, pp, n_groups, 1),
        out_specs=pl.BlockSpec((1, t, A_W), lambda rr, ph, g, pt: (rr, 0, 0)),
        scratch_shapes=[pltpu.VMEM((n_pages + 1, m_rows, LANES), F32), pltpu.VMEM((m_rows, A_W), F32),
                        pltpu.VMEM((m_rows, 1), F32)],
    )
    return pl.pallas_call(
        kern, grid_spec=grid_spec, out_shape=jax.ShapeDtypeStruct((r, t, A_W), BF16),
        compiler_params=_params(3), name="moba_sample",
    )(page_table, slopes, qbd, knT, vnT, *([cache_kT] * pp), *([cache_vT] * pp))


def _dsa_sample_kernel(pt_ref, sl_ref, q_ref, iq_ref, wb_ref, iknT_ref, knT_ref, vnT_ref, *rest,
                       pp, n_groups, n_pages, n_tok, past, k_sel, idx_bits):
    i_refs, k_refs, v_refs = rest[:pp], rest[pp:2 * pp], rest[2 * pp:3 * pp]
    o_ref, isc, sc, acc_ref, l_ref = rest[3 * pp:]
    ph = pl.program_id(1)
    g = pl.program_id(2)
    m_rows = B_HEADS * n_tok
    q = q_ref[0]
    iq = iq_ref[0]

    def idx_scores(ikT):
        r = _mm(iq, ikT.astype(BF16))
        out = jnp.zeros((n_tok, LANES), F32)
        for h in range(IDX_HEADS):
            out = out + jnp.maximum(r[h * n_tok:(h + 1) * n_tok], 0.0) * (wb_ref[0, h] * (IDX_W_SCALE * IDX_SCALE))
        return out

    @pl.when(ph == 0)
    def _():
        for i in range(pp):
            isc[g * pp + i] = idx_scores(i_refs[i][...])
            sc[g * pp + i] = _mm(q, k_refs[i][...].astype(BF16))

    @pl.when((ph == 0) & (g == n_groups - 1))
    def _():
        lane = lax.broadcasted_iota(I32, (n_tok, LANES), 1)
        tok = lax.broadcasted_iota(I32, (n_tok, LANES), 0)
        isc[n_pages] = jnp.where(lane <= tok, idx_scores(iknT_ref[0]), NEG_INF)
        sc[n_pages] = _mm(q, knT_ref[0].astype(BF16))

        def count(pred):
            def body(p, c):
                return c + jnp.where(pred(isc[p], p * PAGE_SIZE + lane), 1.0, 0.0)
            c = lax.fori_loop(0, n_pages + 1, body, jnp.zeros((n_tok, LANES), F32))
            return jnp.sum(c, axis=1, keepdims=True)

        thr, cut = _kth_largest(count, k_sel, n_tok, idx_bits)
        slope3 = _slopes3(sl_ref, B_HEADS)
        tpos = past + tok

        def bias_body(p, m):
            kk = isc[p]
            idx = p * PAGE_SIZE + lane
            chosen = ((kk > thr) | ((kk == thr) & (idx <= cut))) & (idx <= tpos)
            dist = (tpos - idx).astype(F32)
            s3 = sc[p].reshape(B_HEADS, n_tok, LANES) * ATTN_SCALE - slope3 * dist[None]
            s = jnp.where(chosen[None], s3, NEG_INF).reshape(m_rows, LANES)
            sc[p] = s
            return jnp.maximum(m, jnp.max(s, axis=1, keepdims=True))

        m = lax.fori_loop(0, n_pages + 1, bias_body, jnp.full((m_rows, 1), M_FLOOR, F32))

        def exp_body(p, l):
            e = jnp.exp(sc[p] - m)
            sc[p] = e
            return l + jnp.sum(e, axis=1, keepdims=True)

        l_ref[...] = lax.fori_loop(0, n_pages + 1, exp_body, jnp.zeros((m_rows, 1), F32))

    @pl.when(ph == 1)
    def _():
        @pl.when(g == 0)
        def _():
            acc_ref[...] = jnp.zeros_like(acc_ref)
        a = acc_ref[...]
        for i in range(pp):
            a = a + _nt(sc[g * pp + i].astype(BF16), v_refs[i][...].astype(BF16))
        acc_ref[...] = a

    @pl.when((ph == 1) & (g == n_groups - 1))
    def _():
        a = acc_ref[...] + _nt(sc[n_pages].astype(BF16), vnT_ref[0].astype(BF16))
        o_ref[0] = _merge_heads(a / l_ref[...], B_HEADS, n_tok, HEAD_DIM).astype(o_ref.dtype)


def _dsa_sample(qb, kb, vb, iq, ik, iw, cache_kT, cache_vT, cache_iT, page_table, layer, slopes):
    r, t, _ = qb.shape
    n_pages = page_table.shape[1]
    past = n_pages * PAGE_SIZE
    pp = min(SAMPLE_PAGES_PER_STEP, n_pages)
    n_groups = n_pages // pp
    m_rows = B_HEADS * t
    kern = functools.partial(_dsa_sample_kernel, pp=pp, n_groups=n_groups, n_pages=n_pages, n_tok=t, past=past,
                             k_sel=min(DSA_TOPK, (past + t) // 4), idx_bits=(past + PAGE_SIZE - 1).bit_length())
    qbd = _block_diag_rows(qb.reshape(r, t, B_HEADS, HEAD_DIM))
    iqs = jnp.transpose(iq.reshape(r, t, IDX_HEADS, IDX_DIM), (0, 2, 1, 3)).reshape(r, IDX_HEADS * t, IDX_DIM)
    iqs = iqs.astype(BF16)
    wbs = jnp.broadcast_to(jnp.transpose(iw, (0, 2, 1))[..., None], (r, IDX_HEADS, t, LANES))
    iknT, knT, vnT = _new_T(ik), _new_T(kb), _new_T(vb)
    per_req = lambda a: pl.BlockSpec((1,) + a.shape[1:], lambda rr, ph, g, pt: (rr,) + (0,) * (a.ndim - 1))
    grid_spec = pltpu.PrefetchScalarGridSpec(
        num_scalar_prefetch=1,
        grid=(r, 2, n_groups),
        in_specs=[_smem(), per_req(qbd), per_req(iqs), per_req(wbs), per_req(iknT), per_req(knT), per_req(vnT)]
        + _page_specs(n_pages, IDX_DIM, layer, pp, n_groups, 0) + _page_specs(n_pages, B_W, layer, pp, n_groups, 0)
        + _page_specs(n_pages, B_W, layer, pp, n_groups, 1),
        out_specs=pl.BlockSpec((1, t, B_W), lambda rr, ph, g, pt: (rr, 0, 0)),
        scratch_shapes=[pltpu.VMEM((n_pages + 1, t, LANES), F32), pltpu.VMEM((n_pages + 1, m_rows, LANES), F32),
                        pltpu.VMEM((m_rows, B_W), F32), pltpu.VMEM((m_rows, 1), F32)],
    )
    return pl.pallas_call(
        kern, grid_spec=grid_spec, out_shape=jax.ShapeDtypeStruct((r, t, B_W), BF16),
        compiler_params=_params(3), name="dsa_sample",
    )(page_table, slopes, qbd, iqs, wbs, iknT, knT, vnT,
      *([cache_iT] * pp), *([cache_kT] * pp), *([cache_vT] * pp))


def _alibi_slopes(n):
    return jnp.exp2(-8.0 * jnp.arange(1, n + 1, dtype=F32) / n)


def _heads_T(xT, heads):
    b, _, s = xT.shape
    return jnp.transpose(xT.reshape(b, heads, HEAD_DIM, s), (0, 3, 1, 2))


def _paged_T(cache):
    if cache.ndim == 5:
        d, n, p, h, dh = cache.shape
        return jnp.transpose(cache, (0, 1, 3, 4, 2)).reshape(d, n, h * dh, p)
    return jnp.transpose(cache, (0, 1, 3, 2))


def kernel(x_prompt, x_sample, cache_a_k, cache_a_v, cache_b_k, cache_b_v, cache_idx_k, cache_mem_k, cache_mem_v,
           page_table, mem_prompt, w_in, w_mem_kv, w_out, ln1_g, ln1_b, w_ff1, w_ff2, ln2_g, ln2_b):
    depth = w_in.shape[0]
    bsz, seq, d_model = x_prompt.shape
    r, t, _ = x_sample.shape
    assert seq % KEY_BLOCK == 0 and page_table.shape[1] * PAGE_SIZE % MOBA_BLOCK == 0
    assert page_table.shape[1] * PAGE_SIZE // MOBA_BLOCK <= LANES and seq // MOBA_BLOCK <= LANES
    alpha = (2 * depth) ** 0.25
    slopes_a, slopes_b = _alibi_slopes(A_HEADS), _alibi_slopes(B_HEADS)
    tm_proj = min(512, seq)
    tm_tail = min(512, bsz * seq)

    w_out_b, w_ff1_b, w_ff2_b = w_out.astype(BF16), w_ff1.astype(BF16), w_ff2.astype(BF16)
    w_in_b = w_in.astype(BF16)
    vec = lambda a, l: a[l].reshape(1, d_model)

    def tail(x2d, oa, ob, om, l, tm):
        n = x2d.shape[0]
        return _tail(x2d, oa.reshape(n, A_W), ob.reshape(n, B_W), om.reshape(n, M_W), w_out_b[l], vec(ln1_g, l),
                     vec(ln1_b, l), w_ff1_b[l], w_ff2_b[l], vec(ln2_g, l), vec(ln2_b, l), alpha, tm)

    x = x_prompt
    p_states = []
    for l in range(depth):
        wn, wt = _split_w_in(w_in[l])
        qa, qb, iq, qm, iw, kaT, vaT, kbT, vbT, ikT = _proj_prompt(x, wn, wt, tm_proj)
        kvT = _mem_kv(mem_prompt, w_mem_kv[l].T.astype(BF16))
        kmT, vmT = kvT[:, :M_W], kvT[:, M_W:]
        o_a = _moba_prompt(qa, kaT, vaT, slopes_a)
        o_b = _dsa_prompt(qb, iq, iw, ikT, kbT, vbT, slopes_b)
        o_m = _mem_attn(qm, kmT, vmT, KEY_BLOCK)
        x = tail(x.reshape(bsz * seq, d_model), o_a, o_b, o_m, l, tm_tail).reshape(bsz, seq, d_model)
        p_states.append((_heads_T(kaT, A_HEADS), _heads_T(vaT, A_HEADS), _heads_T(kbT, B_HEADS),
                         _heads_T(vbT, B_HEADS), jnp.transpose(ikT, (0, 2, 1)),
                         _heads_T(kmT, M_HEADS), _heads_T(vmT, M_HEADS)))
    y_prompt = x

    caT, cvT = _paged_T(cache_a_k), _paged_T(cache_a_v)
    cbkT, cbvT, ciT = _paged_T(cache_b_k), _paged_T(cache_b_v), _paged_T(cache_idx_k)
    n_mem = cache_mem_k.shape[2]
    mem_T = lambda c: jnp.transpose(c, (0, 2, 3, 1)).reshape(r, M_W, n_mem)
    o = _OFF
    x = x_sample.reshape(r * t, d_model)
    s_states = []
    for l in range(depth):
        p = _matmul(x, w_in_b[l]).reshape(r, t, -1)
        qa, ka, va, qb, kb, vb, iq, ik, iw, qm = [p[..., o[i]:o[i + 1]] for i in range(10)]
        o_a = _moba_sample(qa, ka, va, caT, cvT, page_table, l, slopes_a)
        o_b = _dsa_sample(qb, kb, vb, iq, ik, iw, cbkT, cbvT, ciT, page_table, l, slopes_b)
        o_m = _mem_attn(qm, mem_T(cache_mem_k[l]), mem_T(cache_mem_v[l]), t)
        x = tail(x, o_a, o_b, o_m, l, r * t)
        s_states.append((ka.reshape(r, t, A_HEADS, HEAD_DIM), va.reshape(r, t, A_HEADS, HEAD_DIM),
                         kb.reshape(r, t, B_HEADS, HEAD_DIM), vb.reshape(r, t, B_HEADS, HEAD_DIM), ik))
    y_sample = x.reshape(r, t, d_model)

    p_out = [jnp.stack(z, axis=0) for z in zip(*p_states)]
    s_out = [jnp.stack(z, axis=0) for z in zip(*s_states)]
    return (y_prompt, y_sample, *p_out, *s_out)
```

```python
import functools
import math

import jax
import jax.numpy as jnp
from jax import lax
from jax.experimental import pallas as pl
from jax.experimental.pallas import tpu as pltpu

F32, BF16, I32 = jnp.float32, jnp.bfloat16, jnp.int32

HEAD_DIM = 64
A_HEADS, B_HEADS, M_HEADS = 8, 4, 4
IDX_HEADS, IDX_DIM = 8, 32
A_W, B_W, M_W = A_HEADS * HEAD_DIM, B_HEADS * HEAD_DIM, M_HEADS * HEAD_DIM
IDX_W = IDX_HEADS * IDX_DIM
MOBA_BLOCK, MOBA_TOPK, DSA_TOPK, PAGE_SIZE = 256, 3, 256, 128
LN_EPS = 1e-5
ATTN_SCALE = HEAD_DIM ** -0.5
IDX_SCALE = IDX_DIM ** -0.5
IDX_W_SCALE = IDX_HEADS ** -0.5
_SPLITS = (A_W, A_W, A_W, B_W, B_W, B_W, IDX_W, IDX_DIM, IDX_HEADS, M_W)
_OFF = tuple(sum(_SPLITS[:i]) for i in range(len(_SPLITS) + 1))

LANES = 128
KEY_BLOCK = 256
ROW_CHUNK = 256
NEG_INF = float("-inf")
M_FLOOR = -1e30
VMEM_LIMIT = 56 * 1024 * 1024
SAMPLE_PAGES_PER_STEP = 16


def _mm(a, b):
    return jnp.dot(a, b, preferred_element_type=F32)


def _nt(a, b):
    return lax.dot_general(a, b, (((1,), (1,)), ((), ())), preferred_element_type=F32)


def _params(n_grid_dims):
    return pltpu.CompilerParams(dimension_semantics=("arbitrary",) * n_grid_dims,
                                vmem_limit_bytes=VMEM_LIMIT)


def _smem():
    return pl.BlockSpec(memory_space=pltpu.SMEM)


def _slopes3(sl_ref, n):
    i = lax.broadcasted_iota(I32, (n, 1, 1), 0)
    out = jnp.full((n, 1, 1), sl_ref[0], F32)
    for h in range(1, n):
        out = jnp.where(i == h, sl_ref[h], out)
    return out


def _stack_heads(q2, n_heads, width):
    lane = lax.broadcasted_iota(I32, q2.shape, 1)
    zero = jnp.zeros_like(q2)
    return jnp.concatenate(
        [jnp.where((lane >= h * width) & (lane < (h + 1) * width), q2, zero) for h in range(n_heads)], axis=0)


def _merge_heads(x, n_heads, t, width):
    lane = lax.broadcasted_iota(I32, (t, n_heads * width), 1)
    out = x[0:t]
    for h in range(1, n_heads):
        out = jnp.where(lane >= h * width, x[h * t:(h + 1) * t], out)
    return out


def _top_select(gate, n_sel):
    blk = lax.broadcasted_iota(I32, gate.shape, 1)
    sel = jnp.zeros(gate.shape, F32)
    g = gate
    for _ in range(n_sel):
        m = jnp.max(g, axis=1, keepdims=True)
        first = jnp.min(jnp.where(g == m, blk, jnp.int32(1 << 30)), axis=1, keepdims=True)
        pick = (blk == first) & (m > NEG_INF)
        sel = jnp.where(pick, 1.0, sel)
        g = jnp.where(pick, NEG_INF, g)
    return sel


def _key_to_f32(key):
    bits = jnp.where(key < 0, key ^ jnp.int32(0x7FFFFFFF), key)
    return lax.bitcast_convert_type(bits, F32)


def _kth_largest(count, k, shape, idx_bits):
    kf = jnp.float32(k)
    int_min = jnp.int32(-2 ** 31)
    int_max = jnp.int32(2 ** 31 - 1)
    ge0 = count(lambda v, i: v >= 0.0)
    key = jnp.where(ge0 >= kf, jnp.zeros(shape, I32), jnp.full(shape, int_min, I32))
    done = jnp.where(ge0 == kf, 1.0, 0.0)
    exact = jnp.zeros(shape, F32)

    def vcond(c):
        return (c[0] < 31) & (c[1] > 0.0)

    def vbody(c):
        b, _, key, done, exact = c
        cand = key | lax.shift_left(jnp.int32(1), jnp.int32(30) - b)
        cf = _key_to_f32(cand)
        cnt = count(lambda v, i: v >= cf)
        key = jnp.where(cnt >= kf, cand, key)
        hit = (cnt == kf) & (done < 0.5)
        exact = jnp.where(hit, cf, exact)
        done = jnp.where(hit, 1.0, done)
        return b + 1, jnp.sum(1.0 - done), key, done, exact

    _, n_open, key, done, exact = lax.while_loop(
        vcond, vbody, (jnp.int32(0), jnp.sum(1.0 - done), key, done, exact))
    thr = _key_to_f32(key)
    thr = jnp.where(thr != thr, NEG_INF, thr)
    thr = jnp.where(done > 0.5, exact, thr)

    def tie_cut():
        need = kf - count(lambda v, i: v > thr)

        def ibody(b, cut):
            cand = cut | lax.shift_left(jnp.int32(1), jnp.int32(idx_bits - 1) - b)
            below = count(lambda v, i: (v == thr) & (i < cand))
            return jnp.where(below < need, cand, cut)

        cut = lax.fori_loop(0, idx_bits, ibody, jnp.zeros(shape, I32))
        return jnp.where(done > 0.5, int_max, cut)

    cut = lax.cond(n_open > 0.0, tie_cut, lambda: jnp.full(shape, int_max, I32))
    return thr, cut


def _layer_norm(x, g, b):
    mu = jnp.mean(x, axis=-1, keepdims=True)
    xc = x - mu
    var = jnp.mean(xc * xc, axis=-1, keepdims=True)
    return xc * lax.rsqrt(var + LN_EPS) * g + b


_NAT_W = A_W + B_W + IDX_W + M_W + LANES
_TR_W = 2 * A_W + 2 * B_W + IDX_DIM


def _split_w_in(w):
    o = _OFF
    pad = jnp.zeros((w.shape[0], LANES - IDX_HEADS), w.dtype)
    wn = jnp.concatenate([w[:, o[0]:o[1]], w[:, o[3]:o[4]], w[:, o[6]:o[7]], w[:, o[9]:o[10]],
                          w[:, o[8]:o[9]], pad], axis=1)
    wt = jnp.concatenate([w[:, o[1]:o[2]], w[:, o[2]:o[3]], w[:, o[4]:o[5]], w[:, o[5]:o[6]],
                          w[:, o[7]:o[8]]], axis=1).T
    return wn.astype(BF16), wt.astype(BF16)


def _proj_prompt_kernel(x_ref, wn_ref, wt_ref, qa_ref, qb_ref, iq_ref, qm_ref, iw_ref,
                        kaT_ref, vaT_ref, kbT_ref, vbT_ref, ikT_ref):
    xb = x_ref[0].astype(BF16)
    nat = _mm(xb, wn_ref[...])
    c = 0
    for ref, w in ((qa_ref, A_W), (qb_ref, B_W), (iq_ref, IDX_W), (qm_ref, M_W)):
        ref[0] = nat[:, c:c + w].astype(BF16)
        c += w
    iw_ref[0] = nat[:, c:c + LANES]
    tr = _nt(wt_ref[...], xb)
    c = 0
    for ref, w in ((kaT_ref, A_W), (vaT_ref, A_W), (kbT_ref, B_W), (vbT_ref, B_W), (ikT_ref, IDX_DIM)):
        ref[0] = tr[c:c + w]
        c += w


def _proj_prompt(x, wn, wt, tm):
    bsz, seq, d = x.shape
    nat = lambda w: pl.BlockSpec((1, tm, w), lambda b, i: (b, i, 0))
    tr = lambda w: pl.BlockSpec((1, w, tm), lambda b, i: (b, 0, i))
    full = lambda a: pl.BlockSpec(a.shape, lambda b, i: (0, 0))
    return pl.pallas_call(
        _proj_prompt_kernel,
        grid=(bsz, seq // tm),
        in_specs=[nat(d), full(wn), full(wt)],
        out_specs=[nat(A_W), nat(B_W), nat(IDX_W), nat(M_W), nat(LANES),
                   tr(A_W), tr(A_W), tr(B_W), tr(B_W), tr(IDX_DIM)],
        out_shape=[jax.ShapeDtypeStruct((bsz, seq, w), BF16) for w in (A_W, B_W, IDX_W, M_W)]
        + [jax.ShapeDtypeStruct((bsz, seq, LANES), F32)]
        + [jax.ShapeDtypeStruct((bsz, w, seq), F32) for w in (A_W, A_W, B_W, B_W, IDX_DIM)],
        compiler_params=_params(2),
        name="proj_prompt",
    )(x, wn, wt)


def _mm_w(a, w):
    if w.dtype == F32:
        return jnp.dot(a.astype(F32), w, preferred_element_type=F32, precision=lax.Precision.HIGHEST)
    return _mm(a.astype(BF16), w)


def _matmul_kernel(x_ref, w_ref, o_ref):
    o_ref[...] = _mm_w(x_ref[...], w_ref[...])


def _matmul(x, w):
    m, n = x.shape[0], w.shape[1]
    return pl.pallas_call(
        _matmul_kernel,
        grid=(1,),
        in_specs=[pl.BlockSpec(x.shape, lambda i: (0, 0)), pl.BlockSpec(w.shape, lambda i: (0, 0))],
        out_specs=pl.BlockSpec((m, n), lambda i: (0, 0)),
        out_shape=jax.ShapeDtypeStruct((m, n), F32),
        compiler_params=_params(1),
        name="proj_sample",
    )(x, w)


def _mem_kv_kernel(wT_ref, m_ref, o_ref):
    o_ref[0] = _nt(wT_ref[...], m_ref[0].astype(BF16))


def _mem_kv(mem, wT):
    bsz, n_mem, d = mem.shape
    return pl.pallas_call(
        _mem_kv_kernel,
        grid=(bsz,),
        in_specs=[pl.BlockSpec(wT.shape, lambda b: (0, 0)), pl.BlockSpec((1, n_mem, d), lambda b: (b, 0, 0))],
        out_specs=pl.BlockSpec((1, wT.shape[0], n_mem), lambda b: (b, 0, 0)),
        out_shape=jax.ShapeDtypeStruct((bsz, wT.shape[0], n_mem), F32),
        compiler_params=_params(1),
        name="mem_kv",
    )(wT, mem)


def _moba_prompt_kernel(sl_ref, q_ref, kT_ref, vT_ref, o_ref, ks, vs, mT, *, tq, nb, n_sel):
    hp = pl.program_id(1)
    qi = pl.program_id(2)
    pair_w = 2 * HEAD_DIM

    @pl.when(qi == 0)
    def _():
        lane = lax.broadcasted_iota(I32, (pair_w, LANES), 1)
        means = jnp.zeros((pair_w, LANES), F32)
        for j in range(nb):
            kblk = kT_ref[0, :, j * KEY_BLOCK:(j + 1) * KEY_BLOCK]
            ks[j] = kblk.astype(BF16)
            vs[j] = vT_ref[0, :, j * KEY_BLOCK:(j + 1) * KEY_BLOCK].astype(BF16)
            means = jnp.where(lane == j, jnp.mean(kblk, axis=1, keepdims=True), means)
        mT[...] = means.astype(BF16)

    m_rows = 2 * tq
    qs = _stack_heads(q_ref[0], 2, HEAD_DIM)
    n_past = qi
    gate = _mm(qs, mT[...])
    blk = lax.broadcasted_iota(I32, (m_rows, LANES), 1)
    gate = jnp.where(blk < n_past, gate, NEG_INF)
    sel = _top_select(gate, n_sel).astype(BF16)

    row = lax.broadcasted_iota(I32, (m_rows, KEY_BLOCK), 0)
    lane = lax.broadcasted_iota(I32, (m_rows, KEY_BLOCK), 1)
    rel = jnp.where(row >= tq, row - tq, row) - lane
    rowc = lax.broadcasted_iota(I32, (m_rows, 1), 0)
    slope = jnp.where(rowc >= tq, sl_ref[2 * hp + 1], sl_ref[2 * hp])

    def scores(j):
        dist = rel + (qi - j) * KEY_BLOCK
        s = _mm(qs, ks[j]) * ATTN_SCALE - slope * dist.astype(F32)
        return s, dist

    s, dist = scores(qi)
    s = jnp.where(dist >= 0, s, NEG_INF)
    m = jnp.max(s, axis=1, keepdims=True)
    p = jnp.exp(s - m)
    l = jnp.sum(p, axis=1, keepdims=True)
    acc = _nt(p.astype(BF16), vs[qi])

    def body(j, carry):
        m, l, acc = carry
        s, _ = scores(j)
        onehot = jnp.where(lax.broadcasted_iota(I32, (LANES, KEY_BLOCK), 0) == j, 1.0, 0.0).astype(BF16)
        chosen = _mm(sel, onehot)
        s = jnp.where(chosen > 0.5, s, NEG_INF)
        m_new = jnp.maximum(m, jnp.max(s, axis=1, keepdims=True))
        a = jnp.exp(m - m_new)
        p = jnp.exp(s - m_new)
        l = a * l + jnp.sum(p, axis=1, keepdims=True)
        acc = a * acc + _nt(p.astype(BF16), vs[j])
        return m_new, l, acc

    m, l, acc = lax.fori_loop(0, n_past, body, (m, l, acc))
    o_ref[0] = _merge_heads(acc / l, 2, tq, HEAD_DIM).astype(o_ref.dtype)


def _moba_prompt(qa, kaT, vaT, slopes):
    bsz, seq, _ = qa.shape
    tq = MOBA_BLOCK
    nb = seq // KEY_BLOCK
    pair_w = 2 * HEAD_DIM
    kern = functools.partial(_moba_prompt_kernel, tq=tq, nb=nb, n_sel=min(MOBA_TOPK, nb))
    kv = pl.BlockSpec((1, pair_w, seq), lambda b, hp, qi: (b, hp, 0))
    return pl.pallas_call(
        kern,
        grid=(bsz, A_HEADS // 2, seq // tq),
        in_specs=[_smem(), pl.BlockSpec((1, tq, pair_w), lambda b, hp, qi: (b, qi, hp)), kv, kv],
        out_specs=pl.BlockSpec((1, tq, pair_w), lambda b, hp, qi: (b, qi, hp)),
        out_shape=jax.ShapeDtypeStruct((bsz, seq, A_W), BF16),
        scratch_shapes=[pltpu.VMEM((nb, pair_w, KEY_BLOCK), BF16), pltpu.VMEM((nb, pair_w, KEY_BLOCK), BF16),
                        pltpu.VMEM((pair_w, LANES), BF16)],
        compiler_params=_params(3),
        name="moba_prompt",
    )(slopes, qa, kaT, vaT)


def _dsa_prompt_kernel(sl_ref, q_ref, iq_ref, iw_ref, ikT_ref, kT_ref, vT_ref, o_ref,
                       ikr, kbs, vbs, keys, iqm, wb, thr_ref, cut_ref, *, tq, nb, k_sel, idx_bits):
    qi = pl.program_id(1)

    @pl.when(qi == 0)
    def _():
        for j in range(nb):
            cols = slice(j * KEY_BLOCK, (j + 1) * KEY_BLOCK)
            ik = ikT_ref[0, :, cols].astype(BF16)
            ikr[j] = jnp.concatenate([ik] * IDX_HEADS, axis=0)
            kbs[j] = kT_ref[0, :, cols].astype(BF16)
            vbs[j] = vT_ref[0, :, cols].astype(BF16)

    t0 = qi * tq
    nkb = qi + 1
    iqm[...] = _stack_heads(iq_ref[0], IDX_HEADS, IDX_DIM)
    w = iw_ref[0] * (IDX_W_SCALE * IDX_SCALE)
    for h in range(IDX_HEADS):
        wb[h] = jnp.broadcast_to(w[:, h:h + 1], (tq, KEY_BLOCK))

    row = lax.broadcasted_iota(I32, (tq, KEY_BLOCK), 0)
    lane = lax.broadcasted_iota(I32, (tq, KEY_BLOCK), 1)
    tpos = t0 + row

    def idx_body(jb, carry):
        r = _mm(iqm[...], ikr[jb])
        sc = jnp.zeros((tq, KEY_BLOCK), F32)
        for h in range(IDX_HEADS):
            sc = sc + jnp.maximum(r[h * tq:(h + 1) * tq], 0.0) * wb[h]
        keys[jb] = jnp.where(jb * KEY_BLOCK + lane <= tpos, sc, NEG_INF)
        return carry

    lax.fori_loop(0, nkb, idx_body, 0)

    lane_c = lax.broadcasted_iota(I32, (ROW_CHUNK, KEY_BLOCK), 1)
    for rc in range(tq // ROW_CHUNK):
        rows = pl.ds(rc * ROW_CHUNK, ROW_CHUNK)

        def count(pred, rows=rows):
            def body(jb, c):
                hit = jnp.where(pred(keys[jb, rows, :], jb * KEY_BLOCK + lane_c), 1.0, 0.0)
                return c + hit[:, :LANES] + hit[:, LANES:]
            c = lax.fori_loop(0, nkb, body, jnp.zeros((ROW_CHUNK, LANES), F32))
            return jnp.sum(c, axis=1, keepdims=True)

        thr, cut = _kth_largest(count, k_sel, (ROW_CHUNK, 1), idx_bits)
        thr_ref[rows, :] = thr
        cut_ref[rows, :] = cut

    thr = thr_ref[...]
    cut = cut_ref[...]
    qs = _stack_heads(q_ref[0], B_HEADS, HEAD_DIM)
    slope3 = _slopes3(sl_ref, B_HEADS)
    m_rows = B_HEADS * tq

    def att_body(jb, carry):
        m, l, acc = carry
        kk = keys[jb]
        idx = jb * KEY_BLOCK + lane
        chosen = ((kk > thr) | ((kk == thr) & (idx <= cut))) & (idx <= tpos)
        dist = (tpos - idx).astype(F32)
        s3 = _mm(qs, kbs[jb]).reshape(B_HEADS, tq, KEY_BLOCK) * ATTN_SCALE - slope3 * dist[None]
        s = jnp.where(chosen[None], s3, NEG_INF).reshape(m_rows, KEY_BLOCK)
        m_new = jnp.maximum(m, jnp.max(s, axis=1, keepdims=True))
        a = jnp.exp(m - m_new)
        p = jnp.exp(s - m_new)
        l = a * l + jnp.sum(p, axis=1, keepdims=True)
        acc = a * acc + _nt(p.astype(BF16), vbs[jb])
        return m_new, l, acc

    init = (jnp.full((m_rows, 1), M_FLOOR, F32), jnp.zeros((m_rows, 1), F32), jnp.zeros((m_rows, B_W), F32))
    m, l, acc = lax.fori_loop(0, nkb, att_body, init)
    o_ref[0] = _merge_heads(acc / l, B_HEADS, tq, HEAD_DIM).astype(o_ref.dtype)


def _dsa_prompt(qb, iq, iw, ikT, kbT, vbT, slopes):
    bsz, seq, _ = qb.shape
    tq = KEY_BLOCK
    nb = seq // KEY_BLOCK
    k_sel = min(DSA_TOPK, seq // 4)
    kern = functools.partial(_dsa_prompt_kernel, tq=tq, nb=nb, k_sel=k_sel, idx_bits=(seq - 1).bit_length())
    qspec = lambda w: pl.BlockSpec((1, tq, w), lambda b, qi: (b, qi, 0))
    kspec = lambda w: pl.BlockSpec((1, w, seq), lambda b, qi: (b, 0, 0))
    return pl.pallas_call(
        kern,
        grid=(bsz, seq // tq),
        in_specs=[_smem(), qspec(B_W), qspec(IDX_W), qspec(LANES), kspec(IDX_DIM), kspec(B_W), kspec(B_W)],
        out_specs=qspec(B_W),
        out_shape=jax.ShapeDtypeStruct((bsz, seq, B_W), BF16),
        scratch_shapes=[pltpu.VMEM((nb, IDX_W, KEY_BLOCK), BF16), pltpu.VMEM((nb, B_W, KEY_BLOCK), BF16),
                        pltpu.VMEM((nb, B_W, KEY_BLOCK), BF16), pltpu.VMEM((nb, tq, KEY_BLOCK), F32),
                        pltpu.VMEM((IDX_HEADS * tq, IDX_W), BF16), pltpu.VMEM((IDX_HEADS, tq, KEY_BLOCK), F32),
                        pltpu.VMEM((tq, 1), F32), pltpu.VMEM((tq, 1), I32)],
        compiler_params=_params(2),
        name="dsa_prompt",
    )(slopes, qb, iq, iw, ikT, kbT, vbT)


def _mem_attn_kernel(q_ref, kT_ref, vT_ref, o_ref, *, tq):
    qs = _stack_heads(q_ref[0].astype(F32), M_HEADS, HEAD_DIM).astype(BF16)
    s = _mm(qs, kT_ref[0].astype(BF16)) * ATTN_SCALE
    p = jnp.exp(s - jnp.max(s, axis=1, keepdims=True))
    l = jnp.sum(p, axis=1, keepdims=True)
    o = _nt(p.astype(BF16), vT_ref[0].astype(BF16)) / l
    o_ref[0] = _merge_heads(o, M_HEADS, tq, HEAD_DIM).astype(o_ref.dtype)


def _mem_attn(qm, kT, vT, tq, out_dtype):
    g, t, _ = qm.shape
    n_mem = kT.shape[2]
    kv = pl.BlockSpec((1, M_W, n_mem), lambda b, i: (b, 0, 0))
    qspec = pl.BlockSpec((1, tq, M_W), lambda b, i: (b, i, 0))
    return pl.pallas_call(
        functools.partial(_mem_attn_kernel, tq=tq),
        grid=(g, t // tq),
        in_specs=[qspec, kv, kv],
        out_specs=qspec,
        out_shape=jax.ShapeDtypeStruct((g, t, M_W), out_dtype),
        compiler_params=_params(2),
        name="mem_attn",
    )(qm, kT, vT)


def _tail_kernel(x_ref, oa_ref, ob_ref, om_ref, wo_ref, g1_ref, b1_ref, w1_ref, w2_ref, g2_ref, b2_ref,
                 y_ref, *, alpha, ff_chunk):
    attn = (_mm_w(oa_ref[...], wo_ref[0:A_W, :]) + _mm_w(ob_ref[...], wo_ref[A_W:A_W + B_W, :])
            + _mm_w(om_ref[...], wo_ref[A_W + B_W:A_W + B_W + M_W, :]))
    x1 = _layer_norm(alpha * x_ref[...] + attn, g1_ref[...], b1_ref[...])
    x1w = x1.astype(w1_ref.dtype)
    d_ff = w1_ref.shape[1]
    acc = jnp.zeros(x1.shape, F32)
    for c in range(0, d_ff, ff_chunk):
        h = jnp.square(jnp.maximum(_mm_w(x1w, w1_ref[:, c:c + ff_chunk]), 0.0))
        acc = acc + _mm_w(h, w2_ref[c:c + ff_chunk, :])
    y_ref[...] = _layer_norm(alpha * x1 + acc, g2_ref[...], b2_ref[...])


def _tail(x, oa, ob, om, wo, g1, b1, w1, w2, g2, b2, alpha, tm):
    n, d = x.shape
    rows = lambda w: pl.BlockSpec((tm, w), lambda i: (i, 0))
    const = lambda a: pl.BlockSpec(a.shape, lambda i: (0, 0), pipeline_mode=pl.Buffered(1))
    return pl.pallas_call(
        functools.partial(_tail_kernel, alpha=alpha, ff_chunk=min(1024, w1.shape[1])),
        grid=(n // tm,),
        in_specs=[rows(d), rows(A_W), rows(B_W), rows(M_W), const(wo), const(g1), const(b1), const(w1),
                  const(w2), const(g2), const(b2)],
        out_specs=rows(d),
        out_shape=jax.ShapeDtypeStruct((n, d), F32),
        compiler_params=_params(1),
        name="tail",
    )(x, oa, ob, om, wo, g1, b1, w1, w2, g2, b2)


def _page_specs(n, width, layer, pp, n_groups, phase_of_use):
    def index(i):
        def f(r, ph, g, pt):
            grp = jnp.where(ph == 0, g, n_groups - 1) if phase_of_use == 0 else jnp.where(ph == 0, 0, g)
            return (layer, pt[r, grp * pp + i], 0, 0)
        return f
    return [pl.BlockSpec((None, None, width, PAGE_SIZE), index(i)) for i in range(pp)]


def _split_bf16(x):
    hi = x.astype(BF16)
    return hi, (x - hi.astype(F32)).astype(BF16)


def _moba_sample_kernel(pt_ref, sl_ref, q_ref, knT_ref, vnT_ref, *rest, pp, n_groups, n_pages, n_tok, past, n_sel):
    k_refs, v_refs = rest[:pp], rest[pp:2 * pp]
    o_ref, sc, acc_ref, l_ref = rest[2 * pp:]
    ph = pl.program_id(1)
    g = pl.program_id(2)
    m_rows = A_HEADS * n_tok
    q2 = q_ref[0]

    def scores(kT):
        k_hi, k_lo = _split_bf16(kT)
        s = _mm(q2, k_hi)
        return s[:m_rows] + s[m_rows:] + _mm(q2[:m_rows], k_lo)

    @pl.when(ph == 0)
    def _():
        for i in range(pp):
            sc[g * pp + i] = scores(k_refs[i][...])

    @pl.when((ph == 0) & (g == n_groups - 1))
    def _():
        sc[n_pages] = scores(knT_ref[0])
        lane = lax.broadcasted_iota(I32, (m_rows, LANES), 1)
        ppb = MOBA_BLOCK // PAGE_SIZE
        n_blocks = n_pages // ppb

        def gate_body(n, gate):
            tot = sc[ppb * n]
            for u in range(1, ppb):
                tot = tot + sc[ppb * n + u]
            return jnp.where(lane == n, jnp.sum(tot, axis=1, keepdims=True) * (1.0 / MOBA_BLOCK), gate)

        gate = lax.fori_loop(0, n_blocks, gate_body, jnp.full((m_rows, LANES), NEG_INF, F32),
                             unroll=math.gcd(n_blocks, 8))
        sel = _top_select(gate, n_sel).astype(BF16)

        slope3 = _slopes3(sl_ref, A_HEADS)
        lane_t = lax.broadcasted_iota(I32, (n_tok, LANES), 1)
        tok = lax.broadcasted_iota(I32, (n_tok, LANES), 0)
        blk_row = lax.broadcasted_iota(I32, (LANES, LANES), 0)

        def bias_body(n, m):
            onehot = jnp.where(blk_row == n, 1.0, 0.0).astype(BF16)
            chosen = _mm(sel, onehot) > 0.5
            for u in range(ppb):
                p = ppb * n + u
                dist = (past + tok - (p * PAGE_SIZE + lane_t)).astype(F32)
                s3 = sc[p].reshape(A_HEADS, n_tok, LANES) * ATTN_SCALE - slope3 * dist[None]
                s = jnp.where(chosen, s3.reshape(m_rows, LANES), NEG_INF)
                sc[p] = s
                m = jnp.maximum(m, s)
            return m

        m = lax.fori_loop(0, n_blocks, bias_body, jnp.full((m_rows, LANES), M_FLOOR, F32),
                          unroll=math.gcd(n_blocks, 4))
        dist = tok - lane_t
        s3 = sc[n_pages].reshape(A_HEADS, n_tok, LANES) * ATTN_SCALE - slope3 * dist.astype(F32)[None]
        s = jnp.where((dist >= 0)[None], s3, NEG_INF).reshape(m_rows, LANES)
        sc[n_pages] = s
        m = jnp.max(jnp.maximum(m, s), axis=1, keepdims=True)

        def exp_body(p, l):
            e = jnp.exp(sc[p] - m)
            sc[p] = e
            return l + e

        l = lax.fori_loop(0, n_pages + 1, exp_body, jnp.zeros((m_rows, LANES), F32))
        l_ref[...] = jnp.sum(l, axis=1, keepdims=True)

    @pl.when(ph == 1)
    def _():
        @pl.when(g == 0)
        def _():
            acc_ref[...] = jnp.zeros_like(acc_ref)
        a = acc_ref[...]
        for i in range(pp):
            a = a + _nt(sc[g * pp + i].astype(BF16), v_refs[i][...].astype(BF16))
        acc_ref[...] = a

    @pl.when((ph == 1) & (g == n_groups - 1))
    def _():
        a = acc_ref[...] + _nt(sc[n_pages].astype(BF16), vnT_ref[0].astype(BF16))
        o_ref[0] = _merge_heads(a / l_ref[...], A_HEADS, n_tok, HEAD_DIM).astype(o_ref.dtype)


def _block_diag_rows(q):
    r, t, h, dh = q.shape
    qh = jnp.transpose(q, (0, 2, 1, 3))
    eye = jnp.eye(h, dtype=bool)[None, :, None, :, None]
    out = jnp.where(eye, qh[:, :, :, None, :], 0.0)
    return out.reshape(r, h * t, h * dh)


def _new_T(x):
    xt = jnp.transpose(x, (0, 2, 1))
    return jnp.pad(xt, ((0, 0), (0, 0), (0, PAGE_SIZE - x.shape[1])))


def _moba_sample(qa, ka, va, cache_kT, cache_vT, page_table, layer, slopes):
    r, t, _ = qa.shape
    n_pages = page_table.shape[1]
    pp = min(SAMPLE_PAGES_PER_STEP, n_pages)
    n_groups = n_pages // pp
    m_rows = A_HEADS * t
    n_blocks = n_pages * PAGE_SIZE // MOBA_BLOCK
    kern = functools.partial(_moba_sample_kernel, pp=pp, n_groups=n_groups, n_pages=n_pages, n_tok=t,
                             past=n_pages * PAGE_SIZE, n_sel=min(MOBA_TOPK, n_blocks))
    per_req = lambda a: pl.BlockSpec((1,) + a.shape[1:], lambda rr, ph, g, pt: (rr, 0, 0))
    qbd = jnp.concatenate(_split_bf16(_block_diag_rows(qa.reshape(r, t, A_HEADS, HEAD_DIM))), axis=1)
    knT, vnT = _new_T(ka), _new_T(va)
    grid_spec = pltpu.PrefetchScalarGridSpec(
        num_scalar_prefetch=1,
        grid=(r, 2, n_groups),
        in_specs=[_smem(), per_req(qbd), per_req(knT), per_req(vnT)]
        + _page_specs(n_pages, A_W, layer, pp, n_groups, 0) + _page_specs(n_pages, A_W, layer, pp, n_groups, 1),
        out_specs=pl.BlockSpec((1, t, A_W), lambda rr, ph, g, pt: (rr, 0, 0)),
        scratch_shapes=[pltpu.VMEM((n_pages + 1, m_rows, LANES), F32), pltpu.VMEM((m_rows, A_W), F32),
                        pltpu.VMEM((m_rows, 1), F32)],
    )
    return pl.pallas_call(
        kern, grid_spec=grid_spec, out_shape=jax.ShapeDtypeStruct((r, t, A_W), F32),
        compiler_params=_params(3), name="moba_sample",
    )(page_table, slopes, qbd, knT, vnT, *([cache_kT] * pp), *([cache_vT] * pp))


def _dsa_sample_scores_kernel(pt_ref, q_ref, iq_ref, wb_ref, iknT_ref, knT_ref, *rest, pp, n_tok):
    i_refs, k_refs = rest[:pp], rest[pp:2 * pp]
    isc_ref, sc_ref, isc_new_ref, sc_new_ref = rest[2 * pp:]
    q = q_ref[0]
    iq = iq_ref[0]

    def idx_scores(ikT):
        r = _mm(iq, ikT.astype(BF16))
        out = jnp.zeros((n_tok, LANES), F32)
        for h in range(IDX_HEADS):
            out = out + jnp.maximum(r[h * n_tok:(h + 1) * n_tok], 0.0) * (wb_ref[0, h] * (IDX_W_SCALE * IDX_SCALE))
        return out

    for i in range(pp):
        isc_ref[i, 0] = idx_scores(i_refs[i][...])
        sc_ref[i, 0] = _mm(q, k_refs[i][...].astype(BF16))
    lane = lax.broadcasted_iota(I32, (n_tok, LANES), 1)
    tok = lax.broadcasted_iota(I32, (n_tok, LANES), 0)
    isc_new_ref[0] = jnp.where(lane <= tok, idx_scores(iknT_ref[0]), NEG_INF)
    sc_new_ref[0] = _mm(q, knT_ref[0].astype(BF16))


def _dsa_sample_select_kernel(isc_ref, isc_new_ref, thr_ref, cut_ref, *, n_pages, k_sel, idx_bits):
    r, t, _ = isc_new_ref.shape
    lane = lax.broadcasted_iota(I32, (r, t, LANES), 2)

    def count(pred):
        def body(p, c):
            return c + jnp.where(pred(isc_ref[p], p * PAGE_SIZE + lane), 1.0, 0.0)
        c = lax.fori_loop(0, n_pages, body, jnp.zeros((r, t, LANES), F32), unroll=math.gcd(n_pages, 4))
        c = c + jnp.where(pred(isc_new_ref[...], n_pages * PAGE_SIZE + lane), 1.0, 0.0)
        return jnp.sum(c, axis=2, keepdims=True)

    thr, cut = _kth_largest(count, k_sel, (r, t, 1), idx_bits)
    thr_ref[...] = jnp.broadcast_to(thr, (r, t, LANES))
    cut_ref[...] = jnp.broadcast_to(cut, (r, t, LANES))


def _dsa_sample_attend_kernel(pt_ref, sl_ref, thr_ref, cut_ref, isc_ref, sc_ref, isc_new_ref, sc_new_ref, vnT_ref,
                              *rest, pp, n_groups, n_pages, n_tok, past):
    v_refs = rest[:pp]
    o_ref, m_ref, l_ref, acc_ref = rest[pp:]
    g = pl.program_id(1)
    m_rows = B_HEADS * n_tok
    thr, cut = thr_ref[0], cut_ref[0]
    slope3 = _slopes3(sl_ref, B_HEADS)
    lane = lax.broadcasted_iota(I32, (n_tok, LANES), 1)
    tpos = past + lax.broadcasted_iota(I32, (n_tok, LANES), 0)

    def biased(kk, raw, first):
        idx = first + lane
        chosen = ((kk > thr) | ((kk == thr) & (idx <= cut))) & (idx <= tpos)
        dist = (tpos - idx).astype(F32)
        s3 = raw.reshape(B_HEADS, n_tok, LANES) * ATTN_SCALE - slope3 * dist[None]
        return jnp.where(chosen[None], s3, NEG_INF).reshape(m_rows, LANES)

    def accumulate(tiles, values):
        top = tiles[0]
        for s in tiles[1:]:
            top = jnp.maximum(top, s)
        m_old = m_ref[...]
        m_new = jnp.maximum(m_old, jnp.max(top, axis=1, keepdims=True))
        a = jnp.exp(m_old - m_new)
        tot = jnp.zeros((m_rows, LANES), F32)
        acc = a * acc_ref[...]
        for s, v in zip(tiles, values):
            p = jnp.exp(s - m_new)
            tot = tot + p
            acc = acc + _nt(p.astype(BF16), v.astype(BF16))
        m_ref[...] = m_new
        l_ref[...] = a * l_ref[...] + jnp.sum(tot, axis=1, keepdims=True)
        acc_ref[...] = acc

    @pl.when(g == 0)
    def _():
        m_ref[...] = jnp.full(m_ref.shape, M_FLOOR, F32)
        l_ref[...] = jnp.zeros_like(l_ref)
        acc_ref[...] = jnp.zeros_like(acc_ref)
        accumulate([biased(isc_new_ref[0], sc_new_ref[0], n_pages * PAGE_SIZE)], [vnT_ref[0]])

    accumulate([biased(isc_ref[i, 0], sc_ref[i, 0], (g * pp + i) * PAGE_SIZE) for i in range(pp)],
               [v_refs[i][...] for i in range(pp)])

    @pl.when(g == n_groups - 1)
    def _():
        o_ref[0] = _merge_heads(acc_ref[...] / l_ref[...], B_HEADS, n_tok, HEAD_DIM).astype(o_ref.dtype)


def _dsa_sample(qb, kb, vb, iq, ik, iw, cache_kT, cache_vT, cache_iT, page_table, layer, slopes):
    r, t, _ = qb.shape
    n_pages = page_table.shape[1]
    past = n_pages * PAGE_SIZE
    pp = min(SAMPLE_PAGES_PER_STEP, n_pages)
    n_groups = n_pages // pp
    m_rows = B_HEADS * t
    qbd = _block_diag_rows(qb.reshape(r, t, B_HEADS, HEAD_DIM)).astype(BF16)
    iqs = jnp.transpose(iq.reshape(r, t, IDX_HEADS, IDX_DIM), (0, 2, 1, 3)).reshape(r, IDX_HEADS * t, IDX_DIM)
    iqs = iqs.astype(BF16)
    wbs = jnp.broadcast_to(jnp.transpose(iw, (0, 2, 1))[..., None], (r, IDX_HEADS, t, LANES))
    iknT, knT, vnT = _new_T(ik), _new_T(kb), _new_T(vb)
    per_req = lambda a: pl.BlockSpec((1,) + a.shape[1:], lambda rr, g, *_: (rr,) + (0,) * (a.ndim - 1))
    pages = lambda width: [pl.BlockSpec((None, None, width, PAGE_SIZE),
                                        functools.partial(lambda rr, g, pt, i: (layer, pt[rr, g * pp + i], 0, 0), i=i))
                           for i in range(pp)]
    paged = lambda rows: pl.BlockSpec((pp, 1, rows, LANES), lambda rr, g, *_: (g, rr, 0, 0))
    isc_shape = jax.ShapeDtypeStruct((n_pages, r, t, LANES), F32)
    sc_shape = jax.ShapeDtypeStruct((n_pages, r, m_rows, LANES), F32)
    isc_new_shape = jax.ShapeDtypeStruct((r, t, LANES), F32)
    sc_new_shape = jax.ShapeDtypeStruct((r, m_rows, LANES), F32)

    isc, sc, isc_new, sc_new = pl.pallas_call(
        functools.partial(_dsa_sample_scores_kernel, pp=pp, n_tok=t),
        grid_spec=pltpu.PrefetchScalarGridSpec(
            num_scalar_prefetch=1, grid=(r, n_groups),
            in_specs=[per_req(qbd), per_req(iqs), per_req(wbs), per_req(iknT), per_req(knT)]
            + pages(IDX_DIM) + pages(B_W),
            out_specs=[paged(t), paged(m_rows), per_req(isc_new_shape), per_req(sc_new_shape)]),
        out_shape=[isc_shape, sc_shape, isc_new_shape, sc_new_shape],
        compiler_params=_params(2), name="dsa_sample_scores",
    )(page_table, qbd, iqs, wbs, iknT, knT, *([cache_iT] * pp), *([cache_kT] * pp))

    whole = lambda a: pl.BlockSpec(a.shape, lambda i: (0,) * a.ndim, pipeline_mode=pl.Buffered(1))
    thr, cut = pl.pallas_call(
        functools.partial(_dsa_sample_select_kernel, n_pages=n_pages, k_sel=min(DSA_TOPK, (past + t) // 4),
                          idx_bits=(past + PAGE_SIZE - 1).bit_length()),
        grid=(1,),
        in_specs=[whole(isc_shape), whole(isc_new_shape)],
        out_specs=[pl.BlockSpec((r, t, LANES), lambda i: (0, 0, 0))] * 2,
        out_shape=[isc_new_shape, jax.ShapeDtypeStruct((r, t, LANES), I32)],
        compiler_params=_params(1), name="dsa_sample_select",
    )(isc, isc_new)

    return pl.pallas_call(
        functools.partial(_dsa_sample_attend_kernel, pp=pp, n_groups=n_groups, n_pages=n_pages, n_tok=t, past=past),
        grid_spec=pltpu.PrefetchScalarGridSpec(
            num_scalar_prefetch=1, grid=(r, n_groups),
            in_specs=[_smem(), per_req(thr), per_req(cut), paged(t), paged(m_rows), per_req(isc_new),
                      per_req(sc_new), per_req(vnT)] + pages(B_W),
            out_specs=pl.BlockSpec((1, t, B_W), lambda rr, g, *_: (rr, 0, 0)),
            scratch_shapes=[pltpu.VMEM((m_rows, 1), F32), pltpu.VMEM((m_rows, 1), F32),
                            pltpu.VMEM((m_rows, B_W), F32)]),
        out_shape=jax.ShapeDtypeStruct((r, t, B_W), F32),
        compiler_params=_params(2), name="dsa_sample_attend",
    )(page_table, slopes, thr, cut, isc, sc, isc_new, sc_new, vnT, *([cache_vT] * pp))


def _alibi_slopes(n):
    return jnp.exp2(-8.0 * jnp.arange(1, n + 1, dtype=F32) / n)


def _heads_T(xT, heads):
    b, _, s = xT.shape
    return jnp.transpose(xT.reshape(b, heads, HEAD_DIM, s), (0, 3, 1, 2))


def _paged_T(cache):
    if cache.ndim == 5:
        d, n, p, h, dh = cache.shape
        return jnp.transpose(cache, (0, 1, 3, 4, 2)).reshape(d, n, h * dh, p)
    return jnp.transpose(cache, (0, 1, 3, 2))


def kernel(x_prompt, x_sample, cache_a_k, cache_a_v, cache_b_k, cache_b_v, cache_idx_k, cache_mem_k, cache_mem_v,
           page_table, mem_prompt, w_in, w_mem_kv, w_out, ln1_g, ln1_b, w_ff1, w_ff2, ln2_g, ln2_b):
    depth = w_in.shape[0]
    bsz, seq, d_model = x_prompt.shape
    r, t, _ = x_sample.shape
    assert seq % KEY_BLOCK == 0 and page_table.shape[1] * PAGE_SIZE % MOBA_BLOCK == 0
    assert page_table.shape[1] * PAGE_SIZE // MOBA_BLOCK <= LANES and seq // MOBA_BLOCK <= LANES
    alpha = (2 * depth) ** 0.25
    slopes_a, slopes_b = _alibi_slopes(A_HEADS), _alibi_slopes(B_HEADS)
    tm_proj = min(512, seq)
    tm_tail = min(512, bsz * seq)

    vec = lambda a, l: a[l].reshape(1, d_model)

    def tail(x2d, oa, ob, om, l, tm, wdtype):
        n = x2d.shape[0]
        return _tail(x2d, oa.reshape(n, A_W), ob.reshape(n, B_W), om.reshape(n, M_W), w_out[l].astype(wdtype),
                     vec(ln1_g, l), vec(ln1_b, l), w_ff1[l].astype(wdtype), w_ff2[l].astype(wdtype),
                     vec(ln2_g, l), vec(ln2_b, l), alpha, tm)

    x = x_prompt
    p_states = []
    for l in range(depth):
        wn, wt = _split_w_in(w_in[l])
        qa, qb, iq, qm, iw, kaT, vaT, kbT, vbT, ikT = _proj_prompt(x, wn, wt, tm_proj)
        kvT = _mem_kv(mem_prompt, w_mem_kv[l].T.astype(BF16))
        kmT, vmT = kvT[:, :M_W], kvT[:, M_W:]
        o_a = _moba_prompt(qa, kaT, vaT, slopes_a)
        o_b = _dsa_prompt(qb, iq, iw, ikT, kbT, vbT, slopes_b)
        o_m = _mem_attn(qm, kmT, vmT, KEY_BLOCK, BF16)
        x = tail(x.reshape(bsz * seq, d_model), o_a, o_b, o_m, l, tm_tail, BF16).reshape(bsz, seq, d_model)
        p_states.append((_heads_T(kaT, A_HEADS), _heads_T(vaT, A_HEADS), _heads_T(kbT, B_HEADS),
                         _heads_T(vbT, B_HEADS), jnp.transpose(ikT, (0, 2, 1)),
                         _heads_T(kmT, M_HEADS), _heads_T(vmT, M_HEADS)))
    y_prompt = x

    caT, cvT = _paged_T(cache_a_k), _paged_T(cache_a_v)
    cbkT, cbvT, ciT = _paged_T(cache_b_k), _paged_T(cache_b_v), _paged_T(cache_idx_k)
    n_mem = cache_mem_k.shape[2]
    mem_T = lambda c: jnp.transpose(c, (0, 2, 3, 1)).reshape(r, M_W, n_mem)
    o = _OFF
    x = x_sample.reshape(r * t, d_model)
    s_states = []
    for l in range(depth):
        p = _matmul(x, w_in[l]).reshape(r, t, -1)
        qa, ka, va, qb, kb, vb, iq, ik, iw, qm = [p[..., o[i]:o[i + 1]] for i in range(10)]
        o_a = _moba_sample(qa, ka, va, caT, cvT, page_table, l, slopes_a)
        o_b = _dsa_sample(qb, kb, vb, iq, ik, iw, cbkT, cbvT, ciT, page_table, l, slopes_b)
        o_m = _mem_attn(qm, mem_T(cache_mem_k[l]), mem_T(cache_mem_v[l]), t, F32)
        x = tail(x, o_a, o_b, o_m, l, r * t, F32)
        s_states.append((ka.reshape(r, t, A_HEADS, HEAD_DIM), va.reshape(r, t, A_HEADS, HEAD_DIM),
                         kb.reshape(r, t, B_HEADS, HEAD_DIM), vb.reshape(r, t, B_HEADS, HEAD_DIM), ik))
    y_sample = x.reshape(r, t, d_model)

    p_out = [jnp.stack(z, axis=0) for z in zip(*p_states)]
    s_out = [jnp.stack(z, axis=0) for z in zip(*s_states)]
    return (y_prompt, y_sample, *p_out, *s_out)
```

```python
import functools
import math

import jax
import jax.numpy as jnp
from jax import lax
from jax.experimental import pallas as pl
from jax.experimental.pallas import tpu as pltpu

F32, BF16, I32 = jnp.float32, jnp.bfloat16, jnp.int32

HEAD_DIM = 64
A_HEADS, B_HEADS, M_HEADS = 8, 4, 4
IDX_HEADS, IDX_DIM = 8, 32
A_W, B_W, M_W = A_HEADS * HEAD_DIM, B_HEADS * HEAD_DIM, M_HEADS * HEAD_DIM
IDX_W = IDX_HEADS * IDX_DIM
MOBA_BLOCK, MOBA_TOPK, DSA_TOPK, PAGE_SIZE = 256, 3, 256, 128
LN_EPS = 1e-5
ATTN_SCALE = HEAD_DIM ** -0.5
IDX_SCALE = IDX_DIM ** -0.5
IDX_W_SCALE = IDX_HEADS ** -0.5
_SPLITS = (A_W, A_W, A_W, B_W, B_W, B_W, IDX_W, IDX_DIM, IDX_HEADS, M_W)
_OFF = tuple(sum(_SPLITS[:i]) for i in range(len(_SPLITS) + 1))

LANES = 128
KEY_BLOCK = 256
ROW_CHUNK = 256
NEG_INF = float("-inf")
M_FLOOR = -1e30
VMEM_LIMIT = 56 * 1024 * 1024
SAMPLE_PAGES_PER_STEP = 16


def _mm(a, b):
    return jnp.dot(a, b, preferred_element_type=F32)


def _nt(a, b):
    return lax.dot_general(a, b, (((1,), (1,)), ((), ())), preferred_element_type=F32)


def _params(n_grid_dims):
    return pltpu.CompilerParams(dimension_semantics=("arbitrary",) * n_grid_dims,
                                vmem_limit_bytes=VMEM_LIMIT)


def _smem():
    return pl.BlockSpec(memory_space=pltpu.SMEM)


def _slopes3(sl_ref, n):
    i = lax.broadcasted_iota(I32, (n, 1, 1), 0)
    out = jnp.full((n, 1, 1), sl_ref[0], F32)
    for h in range(1, n):
        out = jnp.where(i == h, sl_ref[h], out)
    return out


def _stack_heads(q2, n_heads, width):
    lane = lax.broadcasted_iota(I32, q2.shape, 1)
    zero = jnp.zeros_like(q2)
    return jnp.concatenate(
        [jnp.where((lane >= h * width) & (lane < (h + 1) * width), q2, zero) for h in range(n_heads)], axis=0)


def _merge_heads(x, n_heads, t, width):
    lane = lax.broadcasted_iota(I32, (t, n_heads * width), 1)
    out = x[0:t]
    for h in range(1, n_heads):
        out = jnp.where(lane >= h * width, x[h * t:(h + 1) * t], out)
    return out


def _top_select(gate, n_sel, axis=1):
    blk = lax.broadcasted_iota(I32, gate.shape, axis)
    sel = jnp.zeros(gate.shape, F32)
    g = gate
    for _ in range(n_sel):
        m = jnp.max(g, axis=axis, keepdims=True)
        first = jnp.min(jnp.where(g == m, blk, jnp.int32(1 << 30)), axis=axis, keepdims=True)
        pick = (blk == first) & (m > NEG_INF)
        sel = jnp.where(pick, 1.0, sel)
        g = jnp.where(pick, NEG_INF, g)
    return sel


def _key_to_f32(key):
    bits = jnp.where(key < 0, key ^ jnp.int32(0x7FFFFFFF), key)
    return lax.bitcast_convert_type(bits, F32)


def _kth_largest(count, k, shape, idx_bits):
    kf = jnp.float32(k)
    int_min = jnp.int32(-2 ** 31)
    int_max = jnp.int32(2 ** 31 - 1)
    ge0 = count(lambda v, i: v >= 0.0)
    key = jnp.where(ge0 >= kf, jnp.zeros(shape, I32), jnp.full(shape, int_min, I32))
    done = jnp.where(ge0 == kf, 1.0, 0.0)
    exact = jnp.zeros(shape, F32)

    def vcond(c):
        return (c[0] < 31) & (c[1] > 0.0)

    def vbody(c):
        b, _, key, done, exact = c
        cand = key | lax.shift_left(jnp.int32(1), jnp.int32(30) - b)
        cf = _key_to_f32(cand)
        cnt = count(lambda v, i: v >= cf)
        key = jnp.where(cnt >= kf, cand, key)
        hit = (cnt == kf) & (done < 0.5)
        exact = jnp.where(hit, cf, exact)
        done = jnp.where(hit, 1.0, done)
        return b + 1, jnp.sum(1.0 - done), key, done, exact

    _, n_open, key, done, exact = lax.while_loop(
        vcond, vbody, (jnp.int32(0), jnp.sum(1.0 - done), key, done, exact))
    thr = _key_to_f32(key)
    thr = jnp.where(thr != thr, NEG_INF, thr)
    thr = jnp.where(done > 0.5, exact, thr)

    def tie_cut():
        need = kf - count(lambda v, i: v > thr)

        def ibody(b, cut):
            cand = cut | lax.shift_left(jnp.int32(1), jnp.int32(idx_bits - 1) - b)
            below = count(lambda v, i: (v == thr) & (i < cand))
            return jnp.where(below < need, cand, cut)

        cut = lax.fori_loop(0, idx_bits, ibody, jnp.zeros(shape, I32))
        return jnp.where(done > 0.5, int_max, cut)

    cut = lax.cond(n_open > 0.0, tie_cut, lambda: jnp.full(shape, int_max, I32))
    return thr, cut


def _layer_norm(x, g, b):
    mu = jnp.mean(x, axis=-1, keepdims=True)
    xc = x - mu
    var = jnp.mean(xc * xc, axis=-1, keepdims=True)
    return xc * lax.rsqrt(var + LN_EPS) * g + b


IW_ROWS = 16
_NAT = ((A_W, "k_a"), (B_W, "k_b"), (M_W, "q_mem"), (LANES, "k_idx"))
_TR = ((A_W, F32), (A_W, F32), (B_W, F32), (B_W, F32), (IDX_DIM, F32),
       (A_W, BF16), (B_W, BF16), (IDX_W, BF16), (IW_ROWS, F32))


def _split_w_in(w):
    o = _OFF
    d = w.shape[0]
    wn = jnp.concatenate([w[:, o[1]:o[2]], w[:, o[4]:o[5]], w[:, o[9]:o[10]], w[:, o[7]:o[8]],
                          jnp.zeros((d, LANES - IDX_DIM), w.dtype)], axis=1)
    wt = jnp.concatenate([w[:, o[1]:o[2]], w[:, o[2]:o[3]], w[:, o[4]:o[5]], w[:, o[5]:o[6]], w[:, o[7]:o[8]],
                          w[:, o[0]:o[1]], w[:, o[3]:o[4]], w[:, o[6]:o[7]], w[:, o[8]:o[9]],
                          jnp.zeros((d, IW_ROWS - IDX_HEADS), w.dtype)], axis=1).T
    return wn.astype(BF16), wt.astype(BF16)


def _proj_prompt_kernel(x_ref, wn_ref, wt_ref, *out_refs):
    xb = x_ref[0].astype(BF16)
    nat = _mm(xb, wn_ref[...])
    c = 0
    for ref, (w, _) in zip(out_refs[:len(_NAT)], _NAT):
        ref[0] = nat[:, c:c + w].astype(ref.dtype)
        c += w
    tr = _nt(wt_ref[...], xb)
    c = 0
    for ref, (w, _) in zip(out_refs[len(_NAT):], _TR):
        ref[0] = tr[c:c + w].astype(ref.dtype)
        c += w


def _proj_prompt(x, wn, wt, tm):
    bsz, seq, d = x.shape
    nat = lambda w: pl.BlockSpec((1, tm, w), lambda b, i: (b, i, 0))
    tr = lambda w: pl.BlockSpec((1, w, tm), lambda b, i: (b, 0, i))
    full = lambda a: pl.BlockSpec(a.shape, lambda b, i: (0, 0))
    return pl.pallas_call(
        _proj_prompt_kernel,
        grid=(bsz, seq // tm),
        in_specs=[nat(d), full(wn), full(wt)],
        out_specs=[nat(w) for w, _ in _NAT] + [tr(w) for w, _ in _TR],
        out_shape=[jax.ShapeDtypeStruct((bsz, seq, w), BF16) for w, _ in _NAT]
        + [jax.ShapeDtypeStruct((bsz, w, seq), dt) for w, dt in _TR],
        compiler_params=_params(2),
        name="proj_prompt",
    )(x, wn, wt)


def _mm_w(a, w):
    if w.dtype == F32:
        return jnp.dot(a.astype(F32), w, preferred_element_type=F32, precision=lax.Precision.HIGHEST)
    return _mm(a.astype(BF16), w)


def _matmul_kernel(x_ref, w_ref, o_ref):
    o_ref[...] = _mm_w(x_ref[...], w_ref[...])


def _matmul(x, w):
    m, n = x.shape[0], w.shape[1]
    return pl.pallas_call(
        _matmul_kernel,
        grid=(1,),
        in_specs=[pl.BlockSpec(x.shape, lambda i: (0, 0)), pl.BlockSpec(w.shape, lambda i: (0, 0))],
        out_specs=pl.BlockSpec((m, n), lambda i: (0, 0)),
        out_shape=jax.ShapeDtypeStruct((m, n), F32),
        compiler_params=_params(1),
        name="proj_sample",
    )(x, w)


def _mem_kv_kernel(wT_ref, m_ref, o_ref):
    o_ref[0] = _nt(wT_ref[...], m_ref[0].astype(BF16))


def _mem_kv(mem, wT):
    bsz, n_mem, d = mem.shape
    return pl.pallas_call(
        _mem_kv_kernel,
        grid=(bsz,),
        in_specs=[pl.BlockSpec(wT.shape, lambda b: (0, 0)), pl.BlockSpec((1, n_mem, d), lambda b: (b, 0, 0))],
        out_specs=pl.BlockSpec((1, wT.shape[0], n_mem), lambda b: (b, 0, 0)),
        out_shape=jax.ShapeDtypeStruct((bsz, wT.shape[0], n_mem), F32),
        compiler_params=_params(1),
        name="mem_kv",
    )(wT, mem)


def _block_diag_cols(qT, n_heads):
    hrow = lax.broadcasted_iota(I32, qT.shape, 0) // HEAD_DIM
    zero = jnp.zeros_like(qT)
    return jnp.concatenate([jnp.where(hrow == h, qT, zero) for h in range(n_heads)], axis=1)


def _online_update(state, s, shift0, v_rows):
    m, l, acc = state
    m_new = jnp.maximum(m, jnp.max(s, axis=0, keepdims=True) - shift0)
    a = jnp.exp(m - m_new)
    p = jnp.exp(s - (m_new + shift0))
    return m_new, a * l + jnp.sum(p, axis=0, keepdims=True), a * acc + _mm(v_rows, p.astype(BF16))


def _softmax_init(t):
    return (jnp.full((1, t), M_FLOOR, F32), jnp.zeros((1, t), F32), jnp.zeros((HEAD_DIM, t), F32))


def _moba_prompt_kernel(sl_ref, qT_ref, k_ref, kT_ref, vT_ref, o_ref, vs, mN, pen0, sel_ref, *, tq, nb, n_sel):
    hp = pl.program_id(1)
    qi = pl.program_id(2)
    pair_w = 2 * HEAD_DIM
    row = lax.broadcasted_iota(I32, (KEY_BLOCK, tq), 0)
    lane = lax.broadcasted_iota(I32, (KEY_BLOCK, tq), 1)
    rel = lane - row

    @pl.when(qi == 0)
    def _():
        lane_m = lax.broadcasted_iota(I32, (pair_w, LANES), 1)
        means = jnp.zeros((pair_w, LANES), F32)
        for j in range(nb):
            kblk = kT_ref[0, :, j * KEY_BLOCK:(j + 1) * KEY_BLOCK]
            vs[j] = vT_ref[0, :, j * KEY_BLOCK:(j + 1) * KEY_BLOCK].astype(BF16)
            means = jnp.where(lane_m == j, jnp.mean(kblk, axis=1, keepdims=True), means)
        mN[...] = means.T.astype(BF16)
        relf = rel.astype(F32)
        for h in range(2):
            pen0[h] = sl_ref[2 * hp + h] * relf

    qT = qT_ref[0]
    gate = _mm(mN[...], _block_diag_cols(qT, 2))[:sel_ref.shape[0]]
    blk = lax.broadcasted_iota(I32, gate.shape, 0)
    sel_ref[...] = _top_select(jnp.where(blk < qi, gate, NEG_INF), n_sel, axis=0)

    q_bd = _block_diag_cols(qT * jnp.asarray(ATTN_SCALE, BF16), 2)

    def scores(j):
        k_blk = k_ref[0, pl.ds(pl.multiple_of(j * KEY_BLOCK, KEY_BLOCK), KEY_BLOCK), :]
        return _mm(k_blk, q_bd)

    s_all = scores(qi)
    v_blk = vs[qi]
    states = []
    for h in range(2):
        s = jnp.where(rel >= 0, s_all[:, h * tq:(h + 1) * tq] - pen0[h], NEG_INF)
        states.append(_online_update(_softmax_init(tq), s, 0.0, v_blk[h * HEAD_DIM:(h + 1) * HEAD_DIM]))

    def body(j, states):
        s_all = scores(j)
        v_blk = vs[j]
        off = ((qi - j) * KEY_BLOCK).astype(F32)
        out = []
        for h in range(2):
            chosen = sel_ref[pl.ds(j, 1), h * tq:(h + 1) * tq] > 0.5
            s = jnp.where(chosen, s_all[:, h * tq:(h + 1) * tq] - pen0[h], NEG_INF)
            out.append(_online_update(states[h], s, sl_ref[2 * hp + h] * off,
                                      v_blk[h * HEAD_DIM:(h + 1) * HEAD_DIM]))
        return tuple(out)

    states = lax.fori_loop(0, qi, body, tuple(states))
    oT = jnp.concatenate([acc / l for _, l, acc in states], axis=0)
    o_ref[0] = oT.T.astype(o_ref.dtype)


def _moba_prompt(qaT, ka, kaT, vaT, slopes):
    bsz, _, seq = qaT.shape
    tq = MOBA_BLOCK
    nb = seq // KEY_BLOCK
    pair_w = 2 * HEAD_DIM
    kern = functools.partial(_moba_prompt_kernel, tq=tq, nb=nb, n_sel=min(MOBA_TOPK, nb))
    kv = pl.BlockSpec((1, pair_w, seq), lambda b, hp, qi: (b, hp, 0))
    return pl.pallas_call(
        kern,
        grid=(bsz, A_HEADS // 2, seq // tq),
        in_specs=[_smem(), pl.BlockSpec((1, pair_w, tq), lambda b, hp, qi: (b, hp, qi)),
                  pl.BlockSpec((1, seq, pair_w), lambda b, hp, qi: (b, 0, hp)), kv, kv],
        out_specs=pl.BlockSpec((1, tq, pair_w), lambda b, hp, qi: (b, qi, hp)),
        out_shape=jax.ShapeDtypeStruct((bsz, seq, A_W), BF16),
        scratch_shapes=[pltpu.VMEM((nb, pair_w, KEY_BLOCK), BF16), pltpu.VMEM((LANES, pair_w), BF16),
                        pltpu.VMEM((2, KEY_BLOCK, tq), F32), pltpu.VMEM((-(-nb // 8) * 8, 2 * tq), F32)],
        compiler_params=_params(3),
        name="moba_prompt",
    )(slopes, qaT, ka, kaT, vaT)


def _dsa_prompt_kernel(sl_ref, qT_ref, iqT_ref, iwT_ref, ik_ref, k_ref, vT_ref, o_ref,
                       vbs, keys, pen0, *, tq, nb, k_sel, idx_bits):
    qi = pl.program_id(1)
    row = lax.broadcasted_iota(I32, (KEY_BLOCK, tq), 0)
    lane = lax.broadcasted_iota(I32, (KEY_BLOCK, tq), 1)
    rel = lane - row

    @pl.when(qi == 0)
    def _():
        relf = rel.astype(F32)
        for j in range(nb):
            vbs[j] = vT_ref[0, :, j * KEY_BLOCK:(j + 1) * KEY_BLOCK].astype(BF16)
        for h in range(B_HEADS):
            pen0[h] = sl_ref[h] * relf

    nkb = qi + 1

    iqT = iqT_ref[0]
    zpad = jnp.zeros((LANES - IDX_DIM, tq), BF16)
    iq_all = jnp.concatenate(
        [jnp.concatenate([iqT[h * IDX_DIM:(h + 1) * IDX_DIM], zpad], axis=0) for h in range(IDX_HEADS)], axis=1)
    w = iwT_ref[0] * (IDX_W_SCALE * IDX_SCALE)

    def idx_body(jb, carry):
        ik = ik_ref[0, pl.ds(pl.multiple_of(jb * KEY_BLOCK, KEY_BLOCK), KEY_BLOCK), :]
        r = _mm(ik, iq_all)
        sc = jnp.zeros((KEY_BLOCK, tq), F32)
        for h in range(IDX_HEADS):
            sc = sc + jnp.maximum(r[:, h * tq:(h + 1) * tq], 0.0) * w[h:h + 1, :]
        keys[jb] = jnp.where(rel + (qi - jb) * KEY_BLOCK >= 0, sc, NEG_INF)
        return carry

    lax.fori_loop(0, nkb, idx_body, 0)

    def count(pred):
        def body(jb, c):
            hit = jnp.where(pred(keys[jb], jb * KEY_BLOCK + row), 1.0, 0.0)
            return c + jnp.sum(hit.reshape(KEY_BLOCK // 8, 8, tq), axis=0)
        c = lax.fori_loop(0, nkb, body, jnp.zeros((8, tq), F32))
        return jnp.sum(c, axis=0, keepdims=True)

    thr, cut = _kth_largest(count, k_sel, (1, tq), idx_bits)

    q_bd = _block_diag_cols(qT_ref[0] * jnp.asarray(ATTN_SCALE, BF16), B_HEADS)

    def att_body(jb, states):
        kk = keys[jb]
        off = (qi - jb) * KEY_BLOCK
        chosen = ((kk > thr) | ((kk == thr) & (jb * KEY_BLOCK + row <= cut))) & (rel + off >= 0)
        k_blk = k_ref[0, pl.ds(pl.multiple_of(jb * KEY_BLOCK, KEY_BLOCK), KEY_BLOCK), :]
        s_all = _mm(k_blk, q_bd)
        v_blk = vbs[jb]
        offf = off.astype(F32)
        out = []
        for h in range(B_HEADS):
            s = jnp.where(chosen, s_all[:, h * tq:(h + 1) * tq] - pen0[h], NEG_INF)
            out.append(_online_update(states[h], s, sl_ref[h] * offf, v_blk[h * HEAD_DIM:(h + 1) * HEAD_DIM]))
        return tuple(out)

    states = lax.fori_loop(0, nkb, att_body, tuple(_softmax_init(tq) for _ in range(B_HEADS)))
    oT = jnp.concatenate([acc / l for _, l, acc in states], axis=0)
    o_ref[0] = oT.T.astype(o_ref.dtype)


def _dsa_prompt(qbT, iqT, iwT, ik, kb, vbT, slopes):
    bsz, _, seq = qbT.shape
    tq = KEY_BLOCK
    nb = seq // KEY_BLOCK
    k_sel = min(DSA_TOPK, seq // 4)
    kern = functools.partial(_dsa_prompt_kernel, tq=tq, nb=nb, k_sel=k_sel, idx_bits=(seq - 1).bit_length())
    qspec = lambda w: pl.BlockSpec((1, w, tq), lambda b, qi: (b, 0, qi))
    whole = lambda a: pl.BlockSpec((1,) + a.shape[1:], lambda b, qi: (b, 0, 0))
    return pl.pallas_call(
        kern,
        grid=(bsz, seq // tq),
        in_specs=[_smem(), qspec(B_W), qspec(IDX_W), qspec(IW_ROWS), whole(ik), whole(kb), whole(vbT)],
        out_specs=pl.BlockSpec((1, tq, B_W), lambda b, qi: (b, qi, 0)),
        out_shape=jax.ShapeDtypeStruct((bsz, seq, B_W), BF16),
        scratch_shapes=[pltpu.VMEM((nb, B_W, KEY_BLOCK), BF16), pltpu.VMEM((nb, KEY_BLOCK, tq), F32),
                        pltpu.VMEM((B_HEADS, KEY_BLOCK, tq), F32)],
        compiler_params=_params(2),
        name="dsa_prompt",
    )(slopes, qbT, iqT, iwT, ik, kb, vbT)


def _mem_attn_kernel(q_ref, kT_ref, vT_ref, o_ref, *, tq):
    qs = _stack_heads(q_ref[0].astype(F32), M_HEADS, HEAD_DIM).astype(BF16)
    s = _mm(qs, kT_ref[0].astype(BF16)) * ATTN_SCALE
    p = jnp.exp(s - jnp.max(s, axis=1, keepdims=True))
    l = jnp.sum(p, axis=1, keepdims=True)
    o = _nt(p.astype(BF16), vT_ref[0].astype(BF16)) / l
    o_ref[0] = _merge_heads(o, M_HEADS, tq, HEAD_DIM).astype(o_ref.dtype)


def _mem_attn(qm, kT, vT, tq, out_dtype):
    g, t, _ = qm.shape
    n_mem = kT.shape[2]
    kv = pl.BlockSpec((1, M_W, n_mem), lambda b, i: (b, 0, 0))
    qspec = pl.BlockSpec((1, tq, M_W), lambda b, i: (b, i, 0))
    return pl.pallas_call(
        functools.partial(_mem_attn_kernel, tq=tq),
        grid=(g, t // tq),
        in_specs=[qspec, kv, kv],
        out_specs=qspec,
        out_shape=jax.ShapeDtypeStruct((g, t, M_W), out_dtype),
        compiler_params=_params(2),
        name="mem_attn",
    )(qm, kT, vT)


def _tail_kernel(x_ref, oa_ref, ob_ref, om_ref, wo_ref, g1_ref, b1_ref, w1_ref, w2_ref, g2_ref, b2_ref,
                 y_ref, *, alpha, ff_chunk):
    attn = (_mm_w(oa_ref[...], wo_ref[0:A_W, :]) + _mm_w(ob_ref[...], wo_ref[A_W:A_W + B_W, :])
            + _mm_w(om_ref[...], wo_ref[A_W + B_W:A_W + B_W + M_W, :]))
    x1 = _layer_norm(alpha * x_ref[...] + attn, g1_ref[...], b1_ref[...])
    x1w = x1.astype(w1_ref.dtype)
    d_ff = w1_ref.shape[1]
    acc = jnp.zeros(x1.shape, F32)
    for c in range(0, d_ff, ff_chunk):
        h = jnp.square(jnp.maximum(_mm_w(x1w, w1_ref[:, c:c + ff_chunk]), 0.0))
        acc = acc + _mm_w(h, w2_ref[c:c + ff_chunk, :])
    y_ref[...] = _layer_norm(alpha * x1 + acc, g2_ref[...], b2_ref[...])


def _tail(x, oa, ob, om, wo, g1, b1, w1, w2, g2, b2, alpha, tm):
    n, d = x.shape
    rows = lambda w: pl.BlockSpec((tm, w), lambda i: (i, 0))
    const = lambda a: pl.BlockSpec(a.shape, lambda i: (0, 0), pipeline_mode=pl.Buffered(1))
    return pl.pallas_call(
        functools.partial(_tail_kernel, alpha=alpha, ff_chunk=min(1024, w1.shape[1])),
        grid=(n // tm,),
        in_specs=[rows(d), rows(A_W), rows(B_W), rows(M_W), const(wo), const(g1), const(b1), const(w1),
                  const(w2), const(g2), const(b2)],
        out_specs=rows(d),
        out_shape=jax.ShapeDtypeStruct((n, d), F32),
        compiler_params=_params(1),
        name="tail",
    )(x, oa, ob, om, wo, g1, b1, w1, w2, g2, b2)


def _page_specs(n, width, layer, pp, n_groups, phase_of_use):
    def index(i):
        def f(r, ph, g, pt):
            grp = jnp.where(ph == 0, g, n_groups - 1) if phase_of_use == 0 else jnp.where(ph == 0, 0, g)
            return (layer, pt[r, grp * pp + i], 0, 0)
        return f
    return [pl.BlockSpec((None, None, width, PAGE_SIZE), index(i)) for i in range(pp)]


def _split_bf16(x):
    hi = x.astype(BF16)
    return hi, (x - hi.astype(F32)).astype(BF16)


def _moba_sample_kernel(pt_ref, sl_ref, q_ref, knT_ref, vnT_ref, *rest, pp, n_groups, n_pages, n_tok, past, n_sel):
    k_refs, v_refs = rest[:pp], rest[pp:2 * pp]
    o_ref, sc, acc_ref, l_ref = rest[2 * pp:]
    ph = pl.program_id(1)
    g = pl.program_id(2)
    m_rows = A_HEADS * n_tok
    q2 = q_ref[0]

    def scores(kT):
        k_hi, k_lo = _split_bf16(kT)
        s = _mm(q2, k_hi)
        return s[:m_rows] + s[m_rows:] + _mm(q2[:m_rows], k_lo)

    @pl.when(ph == 0)
    def _():
        for i in range(pp):
            sc[g * pp + i] = scores(k_refs[i][...])

    @pl.when((ph == 0) & (g == n_groups - 1))
    def _():
        sc[n_pages] = scores(knT_ref[0])
        lane = lax.broadcasted_iota(I32, (m_rows, LANES), 1)
        ppb = MOBA_BLOCK // PAGE_SIZE
        n_blocks = n_pages // ppb

        def gate_body(n, gate):
            tot = sc[ppb * n]
            for u in range(1, ppb):
                tot = tot + sc[ppb * n + u]
            return jnp.where(lane == n, jnp.sum(tot, axis=1, keepdims=True) * (1.0 / MOBA_BLOCK), gate)

        gate = lax.fori_loop(0, n_blocks, gate_body, jnp.full((m_rows, LANES), NEG_INF, F32),
                             unroll=math.gcd(n_blocks, 8))
        sel = _top_select(gate, n_sel).astype(BF16)

        slope3 = _slopes3(sl_ref, A_HEADS)
        lane_t = lax.broadcasted_iota(I32, (n_tok, LANES), 1)
        tok = lax.broadcasted_iota(I32, (n_tok, LANES), 0)
        blk_row = lax.broadcasted_iota(I32, (LANES, LANES), 0)

        def bias_body(n, m):
            onehot = jnp.where(blk_row == n, 1.0, 0.0).astype(BF16)
            chosen = _mm(sel, onehot) > 0.5
            for u in range(ppb):
                p = ppb * n + u
                dist = (past + tok - (p * PAGE_SIZE + lane_t)).astype(F32)
                s3 = sc[p].reshape(A_HEADS, n_tok, LANES) * ATTN_SCALE - slope3 * dist[None]
                s = jnp.where(chosen, s3.reshape(m_rows, LANES), NEG_INF)
                sc[p] = s
                m = jnp.maximum(m, s)
            return m

        m = lax.fori_loop(0, n_blocks, bias_body, jnp.full((m_rows, LANES), M_FLOOR, F32),
                          unroll=math.gcd(n_blocks, 4))
        dist = tok - lane_t
        s3 = sc[n_pages].reshape(A_HEADS, n_tok, LANES) * ATTN_SCALE - slope3 * dist.astype(F32)[None]
        s = jnp.where((dist >= 0)[None], s3, NEG_INF).reshape(m_rows, LANES)
        sc[n_pages] = s
        m = jnp.max(jnp.maximum(m, s), axis=1, keepdims=True)

        def exp_body(p, l):
            e = jnp.exp(sc[p] - m)
            sc[p] = e
            return l + e

        l = lax.fori_loop(0, n_pages + 1, exp_body, jnp.zeros((m_rows, LANES), F32))
        l_ref[...] = jnp.sum(l, axis=1, keepdims=True)

    @pl.when(ph == 1)
    def _():
        @pl.when(g == 0)
        def _():
            acc_ref[...] = jnp.zeros_like(acc_ref)
        a = acc_ref[...]
        for i in range(pp):
            a = a + _nt(sc[g * pp + i].astype(BF16), v_refs[i][...].astype(BF16))
        acc_ref[...] = a

    @pl.when((ph == 1) & (g == n_groups - 1))
    def _():
        a = acc_ref[...] + _nt(sc[n_pages].astype(BF16), vnT_ref[0].astype(BF16))
        o_ref[0] = _merge_heads(a / l_ref[...], A_HEADS, n_tok, HEAD_DIM).astype(o_ref.dtype)


def _block_diag_rows(q):
    r, t, h, dh = q.shape
    qh = jnp.transpose(q, (0, 2, 1, 3))
    eye = jnp.eye(h, dtype=bool)[None, :, None, :, None]
    out = jnp.where(eye, qh[:, :, :, None, :], 0.0)
    return out.reshape(r, h * t, h * dh)


def _new_T(x):
    xt = jnp.transpose(x, (0, 2, 1))
    return jnp.pad(xt, ((0, 0), (0, 0), (0, PAGE_SIZE - x.shape[1])))


def _moba_sample(qa, ka, va, cache_kT, cache_vT, page_table, layer, slopes):
    r, t, _ = qa.shape
    n_pages = page_table.shape[1]
    pp = min(SAMPLE_PAGES_PER_STEP, n_pages)
    n_groups = n_pages // pp
    m_rows = A_HEADS * t
    n_blocks = n_pages * PAGE_SIZE // MOBA_BLOCK
    kern = functools.partial(_moba_sample_kernel, pp=pp, n_groups=n_groups, n_pages=n_pages, n_tok=t,
                             past=n_pages * PAGE_SIZE, n_sel=min(MOBA_TOPK, n_blocks))
    per_req = lambda a: pl.BlockSpec((1,) + a.shape[1:], lambda rr, ph, g, pt: (rr, 0, 0))
    qbd = jnp.concatenate(_split_bf16(_block_diag_rows(qa.reshape(r, t, A_HEADS, HEAD_DIM))), axis=1)
    knT, vnT = _new_T(ka), _new_T(va)
    grid_spec = pltpu.PrefetchScalarGridSpec(
        num_scalar_prefetch=1,
        grid=(r, 2, n_groups),
        in_specs=[_smem(), per_req(qbd), per_req(knT), per_req(vnT)]
        + _page_specs(n_pages, A_W, layer, pp, n_groups, 0) + _page_specs(n_pages, A_W, layer, pp, n_groups, 1),
        out_specs=pl.BlockSpec((1, t, A_W), lambda rr, ph, g, pt: (rr, 0, 0)),
        scratch_shapes=[pltpu.VMEM((n_pages + 1, m_rows, LANES), F32), pltpu.VMEM((m_rows, A_W), F32),
                        pltpu.VMEM((m_rows, 1), F32)],
    )
    return pl.pallas_call(
        kern, grid_spec=grid_spec, out_shape=jax.ShapeDtypeStruct((r, t, A_W), F32),
        compiler_params=_params(3), name="moba_sample",
    )(page_table, slopes, qbd, knT, vnT, *([cache_kT] * pp), *([cache_vT] * pp))


def _dsa_sample_scores_kernel(pt_ref, q_ref, iq_ref, wb_ref, iknT_ref, knT_ref, *rest, pp, n_tok):
    i_refs, k_refs = rest[:pp], rest[pp:2 * pp]
    isc_ref, sc_ref, isc_new_ref, sc_new_ref = rest[2 * pp:]
    q = q_ref[0]
    iq2 = iq_ref[0]
    n_iq = IDX_HEADS * n_tok

    def idx_scores(ikT):
        ik_hi, ik_lo = _split_bf16(ikT)
        r = _mm(iq2, ik_hi)
        r = r[:n_iq] + r[n_iq:] + _mm(iq2[:n_iq], ik_lo)
        out = jnp.zeros((n_tok, LANES), F32)
        for h in range(IDX_HEADS):
            out = out + jnp.maximum(r[h * n_tok:(h + 1) * n_tok], 0.0) * (wb_ref[0, h] * (IDX_W_SCALE * IDX_SCALE))
        return out

    for i in range(pp):
        isc_ref[i, 0] = idx_scores(i_refs[i][...])
        sc_ref[i, 0] = _mm(q, k_refs[i][...].astype(BF16))
    lane = lax.broadcasted_iota(I32, (n_tok, LANES), 1)
    tok = lax.broadcasted_iota(I32, (n_tok, LANES), 0)
    isc_new_ref[0] = jnp.where(lane <= tok, idx_scores(iknT_ref[0]), NEG_INF)
    sc_new_ref[0] = _mm(q, knT_ref[0].astype(BF16))


def _dsa_sample_select_kernel(isc_ref, isc_new_ref, thr_ref, cut_ref, *, n_pages, k_sel, idx_bits):
    r, t, _ = isc_new_ref.shape
    lane = lax.broadcasted_iota(I32, (r, t, LANES), 2)

    def count(pred):
        def body(p, c):
            return c + jnp.where(pred(isc_ref[p], p * PAGE_SIZE + lane), 1.0, 0.0)
        c = lax.fori_loop(0, n_pages, body, jnp.zeros((r, t, LANES), F32), unroll=math.gcd(n_pages, 4))
        c = c + jnp.where(pred(isc_new_ref[...], n_pages * PAGE_SIZE + lane), 1.0, 0.0)
        return jnp.sum(c, axis=2, keepdims=True)

    thr, cut = _kth_largest(count, k_sel, (r, t, 1), idx_bits)
    thr_ref[...] = jnp.broadcast_to(thr, (r, t, LANES))
    cut_ref[...] = jnp.broadcast_to(cut, (r, t, LANES))


def _dsa_sample_attend_kernel(pt_ref, sl_ref, thr_ref, cut_ref, isc_ref, sc_ref, isc_new_ref, sc_new_ref, vnT_ref,
                              *rest, pp, n_groups, n_pages, n_tok, past):
    v_refs = rest[:pp]
    o_ref, m_ref, l_ref, acc_ref = rest[pp:]
    g = pl.program_id(1)
    m_rows = B_HEADS * n_tok
    thr, cut = thr_ref[0], cut_ref[0]
    slope3 = _slopes3(sl_ref, B_HEADS)
    lane = lax.broadcasted_iota(I32, (n_tok, LANES), 1)
    tpos = past + lax.broadcasted_iota(I32, (n_tok, LANES), 0)

    def biased(kk, raw, first):
        idx = first + lane
        chosen = ((kk > thr) | ((kk == thr) & (idx <= cut))) & (idx <= tpos)
        dist = (tpos - idx).astype(F32)
        s3 = raw.reshape(B_HEADS, n_tok, LANES) * ATTN_SCALE - slope3 * dist[None]
        return jnp.where(chosen[None], s3, NEG_INF).reshape(m_rows, LANES)

    def accumulate(tiles, values):
        top = tiles[0]
        for s in tiles[1:]:
            top = jnp.maximum(top, s)
        m_old = m_ref[...]
        m_new = jnp.maximum(m_old, jnp.max(top, axis=1, keepdims=True))
        a = jnp.exp(m_old - m_new)
        tot = jnp.zeros((m_rows, LANES), F32)
        acc = a * acc_ref[...]
        for s, v in zip(tiles, values):
            p = jnp.exp(s - m_new)
            tot = tot + p
            acc = acc + _nt(p.astype(BF16), v.astype(BF16))
        m_ref[...] = m_new
        l_ref[...] = a * l_ref[...] + jnp.sum(tot, axis=1, keepdims=True)
        acc_ref[...] = acc

    @pl.when(g == 0)
    def _():
        m_ref[...] = jnp.full(m_ref.shape, M_FLOOR, F32)
        l_ref[...] = jnp.zeros_like(l_ref)
        acc_ref[...] = jnp.zeros_like(acc_ref)
        accumulate([biased(isc_new_ref[0], sc_new_ref[0], n_pages * PAGE_SIZE)], [vnT_ref[0]])

    accumulate([biased(isc_ref[i, 0], sc_ref[i, 0], (g * pp + i) * PAGE_SIZE) for i in range(pp)],
               [v_refs[i][...] for i in range(pp)])

    @pl.when(g == n_groups - 1)
    def _():
        o_ref[0] = _merge_heads(acc_ref[...] / l_ref[...], B_HEADS, n_tok, HEAD_DIM).astype(o_ref.dtype)


def _dsa_sample(qb, kb, vb, iq, ik, iw, cache_kT, cache_vT, cache_iT, page_table, layer, slopes):
    r, t, _ = qb.shape
    n_pages = page_table.shape[1]
    past = n_pages * PAGE_SIZE
    pp = min(SAMPLE_PAGES_PER_STEP, n_pages)
    n_groups = n_pages // pp
    m_rows = B_HEADS * t
    qbd = _block_diag_rows(qb.reshape(r, t, B_HEADS, HEAD_DIM)).astype(BF16)
    iqs = jnp.transpose(iq.reshape(r, t, IDX_HEADS, IDX_DIM), (0, 2, 1, 3)).reshape(r, IDX_HEADS * t, IDX_DIM)
    iqs = jnp.concatenate(_split_bf16(iqs), axis=1)
    wbs = jnp.broadcast_to(jnp.transpose(iw, (0, 2, 1))[..., None], (r, IDX_HEADS, t, LANES))
    iknT, knT, vnT = _new_T(ik), _new_T(kb), _new_T(vb)
    per_req = lambda a: pl.BlockSpec((1,) + a.shape[1:], lambda rr, g, *_: (rr,) + (0,) * (a.ndim - 1))
    pages = lambda width: [pl.BlockSpec((None, None, width, PAGE_SIZE),
                                        functools.partial(lambda rr, g, pt, i: (layer, pt[rr, g * pp + i], 0, 0), i=i))
                           for i in range(pp)]
    paged = lambda rows: pl.BlockSpec((pp, 1, rows, LANES), lambda rr, g, *_: (g, rr, 0, 0))
    isc_shape = jax.ShapeDtypeStruct((n_pages, r, t, LANES), F32)
    sc_shape = jax.ShapeDtypeStruct((n_pages, r, m_rows, LANES), F32)
    isc_new_shape = jax.ShapeDtypeStruct((r, t, LANES), F32)
    sc_new_shape = jax.ShapeDtypeStruct((r, m_rows, LANES), F32)

    isc, sc, isc_new, sc_new = pl.pallas_call(
        functools.partial(_dsa_sample_scores_kernel, pp=pp, n_tok=t),
        grid_spec=pltpu.PrefetchScalarGridSpec(
            num_scalar_prefetch=1, grid=(r, n_groups),
            in_specs=[per_req(qbd), per_req(iqs), per_req(wbs), per_req(iknT), per_req(knT)]
            + pages(IDX_DIM) + pages(B_W),
            out_specs=[paged(t), paged(m_rows), per_req(isc_new_shape), per_req(sc_new_shape)]),
        out_shape=[isc_shape, sc_shape, isc_new_shape, sc_new_shape],
        compiler_params=_params(2), name="dsa_sample_scores",
    )(page_table, qbd, iqs, wbs, iknT, knT, *([cache_iT] * pp), *([cache_kT] * pp))

    whole = lambda a: pl.BlockSpec(a.shape, lambda i: (0,) * a.ndim, pipeline_mode=pl.Buffered(1))
    thr, cut = pl.pallas_call(
        functools.partial(_dsa_sample_select_kernel, n_pages=n_pages, k_sel=min(DSA_TOPK, (past + t) // 4),
                          idx_bits=(past + PAGE_SIZE - 1).bit_length()),
        grid=(1,),
        in_specs=[whole(isc_shape), whole(isc_new_shape)],
        out_specs=[pl.BlockSpec((r, t, LANES), lambda i: (0, 0, 0))] * 2,
        out_shape=[isc_new_shape, jax.ShapeDtypeStruct((r, t, LANES), I32)],
        compiler_params=_params(1), name="dsa_sample_select",
    )(isc, isc_new)

    return pl.pallas_call(
        functools.partial(_dsa_sample_attend_kernel, pp=pp, n_groups=n_groups, n_pages=n_pages, n_tok=t, past=past),
        grid_spec=pltpu.PrefetchScalarGridSpec(
            num_scalar_prefetch=1, grid=(r, n_groups),
            in_specs=[_smem(), per_req(thr), per_req(cut), paged(t), paged(m_rows), per_req(isc_new),
                      per_req(sc_new), per_req(vnT)] + pages(B_W),
            out_specs=pl.BlockSpec((1, t, B_W), lambda rr, g, *_: (rr, 0, 0)),
            scratch_shapes=[pltpu.VMEM((m_rows, 1), F32), pltpu.VMEM((m_rows, 1), F32),
                            pltpu.VMEM((m_rows, B_W), F32)]),
        out_shape=jax.ShapeDtypeStruct((r, t, B_W), F32),
        compiler_params=_params(2), name="dsa_sample_attend",
    )(page_table, slopes, thr, cut, isc, sc, isc_new, sc_new, vnT, *([cache_vT] * pp))


def _alibi_slopes(n):
    return jnp.exp2(-8.0 * jnp.arange(1, n + 1, dtype=F32) / n)


def _heads_T(xT, heads):
    b, _, s = xT.shape
    return jnp.transpose(xT.reshape(b, heads, HEAD_DIM, s), (0, 3, 1, 2))


def _paged_T(cache):
    if cache.ndim == 5:
        d, n, p, h, dh = cache.shape
        return jnp.transpose(cache, (0, 1, 3, 4, 2)).reshape(d, n, h * dh, p)
    return jnp.transpose(cache, (0, 1, 3, 2))


def kernel(x_prompt, x_sample, cache_a_k, cache_a_v, cache_b_k, cache_b_v, cache_idx_k, cache_mem_k, cache_mem_v,
           page_table, mem_prompt, w_in, w_mem_kv, w_out, ln1_g, ln1_b, w_ff1, w_ff2, ln2_g, ln2_b):
    depth = w_in.shape[0]
    bsz, seq, d_model = x_prompt.shape
    r, t, _ = x_sample.shape
    assert seq % KEY_BLOCK == 0 and page_table.shape[1] * PAGE_SIZE % MOBA_BLOCK == 0
    assert page_table.shape[1] * PAGE_SIZE // MOBA_BLOCK <= LANES and seq // MOBA_BLOCK <= LANES
    alpha = (2 * depth) ** 0.25
    slopes_a, slopes_b = _alibi_slopes(A_HEADS), _alibi_slopes(B_HEADS)
    tm_proj = min(512, seq)
    tm_tail = min(512, bsz * seq)

    vec = lambda a, l: a[l].reshape(1, d_model)

    def tail(x2d, oa, ob, om, l, tm, wdtype):
        n = x2d.shape[0]
        return _tail(x2d, oa.reshape(n, A_W), ob.reshape(n, B_W), om.reshape(n, M_W), w_out[l].astype(wdtype),
                     vec(ln1_g, l), vec(ln1_b, l), w_ff1[l].astype(wdtype), w_ff2[l].astype(wdtype),
                     vec(ln2_g, l), vec(ln2_b, l), alpha, tm)

    x = x_prompt
    p_states = []
    for l in range(depth):
        wn, wt = _split_w_in(w_in[l])
        ka, kb, qm, ik, kaT, vaT, kbT, vbT, ikT, qaT, qbT, iqT, iwT = _proj_prompt(x, wn, wt, tm_proj)
        kvT = _mem_kv(mem_prompt, w_mem_kv[l].T.astype(BF16))
        kmT, vmT = kvT[:, :M_W], kvT[:, M_W:]
        o_a = _moba_prompt(qaT, ka, kaT, vaT, slopes_a)
        o_b = _dsa_prompt(qbT, iqT, iwT, ik, kb, vbT, slopes_b)
        o_m = _mem_attn(qm, kmT, vmT, KEY_BLOCK, BF16)
        x = tail(x.reshape(bsz * seq, d_model), o_a, o_b, o_m, l, tm_tail, BF16).reshape(bsz, seq, d_model)
        p_states.append((_heads_T(kaT, A_HEADS), _heads_T(vaT, A_HEADS), _heads_T(kbT, B_HEADS),
                         _heads_T(vbT, B_HEADS), jnp.transpose(ikT, (0, 2, 1)),
                         _heads_T(kmT, M_HEADS), _heads_T(vmT, M_HEADS)))
    y_prompt = x

    caT, cvT = _paged_T(cache_a_k), _paged_T(cache_a_v)
    cbkT, cbvT, ciT = _paged_T(cache_b_k), _paged_T(cache_b_v), _paged_T(cache_idx_k)
    n_mem = cache_mem_k.shape[2]
    mem_T = lambda c: jnp.transpose(c, (0, 2, 3, 1)).reshape(r, M_W, n_mem)
    o = _OFF
    x = x_sample.reshape(r * t, d_model)
    s_states = []
    for l in range(depth):
        p = _matmul(x, w_in[l]).reshape(r, t, -1)
        qa, ka, va, qb, kb, vb, iq, ik, iw, qm = [p[..., o[i]:o[i + 1]] for i in range(10)]
        o_a = _moba_sample(qa, ka, va, caT, cvT, page_table, l, slopes_a)
        o_b = _dsa_sample(qb, kb, vb, iq, ik, iw, cbkT, cbvT, ciT, page_table, l, slopes_b)
        o_m = _mem_attn(qm, mem_T(cache_mem_k[l]), mem_T(cache_mem_v[l]), t, F32)
        x = tail(x, o_a, o_b, o_m, l, r * t, F32)
        s_states.append((ka.reshape(r, t, A_HEADS, HEAD_DIM), va.reshape(r, t, A_HEADS, HEAD_DIM),
                         kb.reshape(r, t, B_HEADS, HEAD_DIM), vb.reshape(r, t, B_HEADS, HEAD_DIM), ik))
    y_sample = x.reshape(r, t, d_model)

    p_out = [jnp.stack(z, axis=0) for z in zip(*p_states)]
    s_out = [jnp.stack(z, axis=0) for z in zip(*s_states)]
    return (y_prompt, y_sample, *p_out, *s_out)
```

```python
import functools
import math

import jax
import jax.numpy as jnp
from jax import lax
from jax.experimental import pallas as pl
from jax.experimental.pallas import tpu as pltpu

F32, BF16, I32 = jnp.float32, jnp.bfloat16, jnp.int32

HEAD_DIM = 64
A_HEADS, B_HEADS, M_HEADS = 8, 4, 4
IDX_HEADS, IDX_DIM = 8, 32
A_W, B_W, M_W = A_HEADS * HEAD_DIM, B_HEADS * HEAD_DIM, M_HEADS * HEAD_DIM
IDX_W = IDX_HEADS * IDX_DIM
MOBA_BLOCK, MOBA_TOPK, DSA_TOPK, PAGE_SIZE = 256, 3, 256, 128
LN_EPS = 1e-5
ATTN_SCALE = HEAD_DIM ** -0.5
IDX_SCALE = IDX_DIM ** -0.5
IDX_W_SCALE = IDX_HEADS ** -0.5
_SPLITS = (A_W, A_W, A_W, B_W, B_W, B_W, IDX_W, IDX_DIM, IDX_HEADS, M_W)
_OFF = tuple(sum(_SPLITS[:i]) for i in range(len(_SPLITS) + 1))

LANES = 128
KEY_BLOCK = 256
ROW_CHUNK = 256
NEG_INF = float("-inf")
M_FLOOR = -1e30
VMEM_LIMIT = 56 * 1024 * 1024
SAMPLE_PAGES_PER_STEP = 32
VALUE_PASSES = 26


def _mm(a, b):
    return jnp.dot(a, b, preferred_element_type=F32)


def _nt(a, b):
    return lax.dot_general(a, b, (((1,), (1,)), ((), ())), preferred_element_type=F32)


def _params(n_grid_dims):
    return pltpu.CompilerParams(dimension_semantics=("arbitrary",) * n_grid_dims,
                                vmem_limit_bytes=VMEM_LIMIT)


def _smem():
    return pl.BlockSpec(memory_space=pltpu.SMEM)


def _slopes3(sl_ref, n):
    i = lax.broadcasted_iota(I32, (n, 1, 1), 0)
    out = jnp.full((n, 1, 1), sl_ref[0], F32)
    for h in range(1, n):
        out = jnp.where(i == h, sl_ref[h], out)
    return out


def _stack_heads(q2, n_heads, width):
    lane = lax.broadcasted_iota(I32, q2.shape, 1)
    zero = jnp.zeros_like(q2)
    return jnp.concatenate(
        [jnp.where((lane >= h * width) & (lane < (h + 1) * width), q2, zero) for h in range(n_heads)], axis=0)


def _merge_heads(x, n_heads, t, width):
    lane = lax.broadcasted_iota(I32, (t, n_heads * width), 1)
    out = x[0:t]
    for h in range(1, n_heads):
        out = jnp.where(lane >= h * width, x[h * t:(h + 1) * t], out)
    return out


def _top_select(gate, n_sel, axis=1):
    blk = lax.broadcasted_iota(I32, gate.shape, axis)
    sel = jnp.zeros(gate.shape, F32)
    g = gate
    for _ in range(n_sel):
        m = jnp.max(g, axis=axis, keepdims=True)
        first = jnp.min(jnp.where(g == m, blk, jnp.int32(1 << 30)), axis=axis, keepdims=True)
        pick = (blk == first) & (m > NEG_INF)
        sel = jnp.where(pick, 1.0, sel)
        g = jnp.where(pick, NEG_INF, g)
    return sel


def _key_to_f32(key):
    bits = jnp.where(key < 0, key ^ jnp.int32(0x7FFFFFFF), key)
    return lax.bitcast_convert_type(bits, F32)


def _kth_largest(count, bounds, k, shape, idx_bits):
    kf = jnp.float32(k)
    int_min = jnp.int32(-2 ** 31)
    int_max = jnp.int32(2 ** 31 - 1)

    def settle(cf, cnt, done, exact):
        hit = (cnt == kf) & (done < 0.5)
        return jnp.where(hit, 1.0, done), jnp.where(hit, cf, exact)

    lo, hi = bounds()

    def acond(c):
        return (c[0] < VALUE_PASSES // 2) & (c[1] > 0.0)

    def abody(c):
        g, _, lo, hi, done, exact = c
        for _ in range(2):
            mid = 0.5 * lo + 0.5 * hi
            cnt = count(lambda v, i: v >= mid)
            done, exact = settle(mid, cnt, done, exact)
            lo = jnp.where(cnt > kf, mid, lo)
            hi = jnp.where(cnt < kf, mid, hi)
        return g + 1, jnp.sum(1.0 - done), lo, hi, done, exact

    zeros = jnp.zeros(shape, F32)
    _, n_open, _, _, done, exact = lax.while_loop(
        acond, abody, (jnp.int32(0), jnp.float32(1.0), lo, hi, zeros, zeros))

    def bitwise(done, exact):
        ge0 = count(lambda v, i: v >= 0.0)
        key = jnp.where(ge0 >= kf, jnp.zeros(shape, I32), jnp.full(shape, int_min, I32))
        done, exact = settle(zeros, ge0, done, exact)

        def vcond(c):
            return (c[0] < 31) & (c[1] > 0.0)

        def vbody(c):
            b, _, key, done, exact = c
            cand = key | lax.shift_left(jnp.int32(1), jnp.int32(30) - b)
            cf = _key_to_f32(cand)
            cnt = count(lambda v, i: v >= cf)
            key = jnp.where(cnt >= kf, cand, key)
            done, exact = settle(cf, cnt, done, exact)
            return b + 1, jnp.sum(1.0 - done), key, done, exact

        _, n_open, key, done, exact = lax.while_loop(
            vcond, vbody, (jnp.int32(0), jnp.sum(1.0 - done), key, done, exact))
        thr = _key_to_f32(key)
        thr = jnp.where(thr != thr, NEG_INF, thr)
        thr = jnp.where(done > 0.5, exact, thr)

        def tie_cut():
            need = kf - count(lambda v, i: v > thr)

            def ibody(b, cut):
                cand = cut | lax.shift_left(jnp.int32(1), jnp.int32(idx_bits - 1) - b)
                below = count(lambda v, i: (v == thr) & (i < cand))
                return jnp.where(below < need, cand, cut)

            cut = lax.fori_loop(0, idx_bits, ibody, jnp.zeros(shape, I32))
            return jnp.where(done > 0.5, int_max, cut)

        return thr, lax.cond(n_open > 0.0, tie_cut, lambda: jnp.full(shape, int_max, I32))

    return lax.cond(n_open > 0.0, bitwise, lambda done, exact: (exact, jnp.full(shape, int_max, I32)),
                    done, exact)


def _layer_norm(x, g, b):
    mu = jnp.mean(x, axis=-1, keepdims=True)
    xc = x - mu
    var = jnp.mean(xc * xc, axis=-1, keepdims=True)
    return xc * lax.rsqrt(var + LN_EPS) * g + b


IW_ROWS = 16
_NAT = ((A_W, "k_a"), (B_W, "k_b"), (M_W, "q_mem"), (LANES, "k_idx"))
_TR = ((A_W, F32), (A_W, F32), (B_W, F32), (B_W, F32), (IDX_DIM, F32),
       (A_W, BF16), (B_W, BF16), (IDX_W, BF16), (IW_ROWS, F32))


def _split_w_in(w):
    o = _OFF
    d = w.shape[0]
    wn = jnp.concatenate([w[:, o[1]:o[2]], w[:, o[4]:o[5]], w[:, o[9]:o[10]], w[:, o[7]:o[8]],
                          jnp.zeros((d, LANES - IDX_DIM), w.dtype)], axis=1)
    wt = jnp.concatenate([w[:, o[1]:o[2]], w[:, o[2]:o[3]], w[:, o[4]:o[5]], w[:, o[5]:o[6]], w[:, o[7]:o[8]],
                          w[:, o[0]:o[1]], w[:, o[3]:o[4]], w[:, o[6]:o[7]], w[:, o[8]:o[9]],
                          jnp.zeros((d, IW_ROWS - IDX_HEADS), w.dtype)], axis=1).T
    return wn.astype(BF16), wt.astype(BF16)


def _proj_prompt_kernel(x_ref, wn_ref, wt_ref, *out_refs):
    xb = x_ref[0].astype(BF16)
    nat = _mm(xb, wn_ref[...])
    c = 0
    for ref, (w, _) in zip(out_refs[:len(_NAT)], _NAT):
        ref[0] = nat[:, c:c + w].astype(ref.dtype)
        c += w
    tr = _nt(wt_ref[...], xb)
    c = 0
    for ref, (w, _) in zip(out_refs[len(_NAT):], _TR):
        ref[0] = tr[c:c + w].astype(ref.dtype)
        c += w


def _proj_prompt(x, wn, wt, tm):
    bsz, seq, d = x.shape
    nat = lambda w: pl.BlockSpec((1, tm, w), lambda b, i: (b, i, 0))
    tr = lambda w: pl.BlockSpec((1, w, tm), lambda b, i: (b, 0, i))
    full = lambda a: pl.BlockSpec(a.shape, lambda b, i: (0, 0))
    return pl.pallas_call(
        _proj_prompt_kernel,
        grid=(bsz, seq // tm),
        in_specs=[nat(d), full(wn), full(wt)],
        out_specs=[nat(w) for w, _ in _NAT] + [tr(w) for w, _ in _TR],
        out_shape=[jax.ShapeDtypeStruct((bsz, seq, w), BF16) for w, _ in _NAT]
        + [jax.ShapeDtypeStruct((bsz, w, seq), dt) for w, dt in _TR],
        compiler_params=_params(2),
        name="proj_prompt",
    )(x, wn, wt)


def _mm_w(a, w):
    if w.dtype == F32:
        return jnp.dot(a.astype(F32), w, preferred_element_type=F32, precision=lax.Precision.HIGHEST)
    return _mm(a.astype(BF16), w)


def _matmul_kernel(x_ref, w_ref, o_ref):
    o_ref[...] = _mm_w(x_ref[...], w_ref[...])


def _matmul(x, w):
    m, n = x.shape[0], w.shape[1]
    return pl.pallas_call(
        _matmul_kernel,
        grid=(1,),
        in_specs=[pl.BlockSpec(x.shape, lambda i: (0, 0)), pl.BlockSpec(w.shape, lambda i: (0, 0))],
        out_specs=pl.BlockSpec((m, n), lambda i: (0, 0)),
        out_shape=jax.ShapeDtypeStruct((m, n), F32),
        compiler_params=_params(1),
        name="proj_sample",
    )(x, w)


def _mem_kv_kernel(wT_ref, m_ref, o_ref):
    o_ref[0] = _nt(wT_ref[...], m_ref[0].astype(BF16))


def _mem_kv(mem, wT):
    bsz, n_mem, d = mem.shape
    return pl.pallas_call(
        _mem_kv_kernel,
        grid=(bsz,),
        in_specs=[pl.BlockSpec(wT.shape, lambda b: (0, 0)), pl.BlockSpec((1, n_mem, d), lambda b: (b, 0, 0))],
        out_specs=pl.BlockSpec((1, wT.shape[0], n_mem), lambda b: (b, 0, 0)),
        out_shape=jax.ShapeDtypeStruct((bsz, wT.shape[0], n_mem), F32),
        compiler_params=_params(1),
        name="mem_kv",
    )(wT, mem)


def _block_diag_cols(qT, n_heads):
    hrow = lax.broadcasted_iota(I32, qT.shape, 0) // HEAD_DIM
    zero = jnp.zeros_like(qT)
    return jnp.concatenate([jnp.where(hrow == h, qT, zero) for h in range(n_heads)], axis=1)


def _online_update(state, s, shift0, v_rows):
    m, l, acc = state
    m_new = jnp.maximum(m, jnp.max(s, axis=0, keepdims=True) - shift0)
    a = jnp.exp(m - m_new)
    p = jnp.exp(s - (m_new + shift0))
    return m_new, a * l + jnp.sum(p, axis=0, keepdims=True), a * acc + _mm(v_rows, p.astype(BF16))


def _softmax_init(t):
    return (jnp.full((1, t), M_FLOOR, F32), jnp.zeros((1, t), F32), jnp.zeros((HEAD_DIM, t), F32))


def _moba_prompt_kernel(sl_ref, qT_ref, k_ref, kT_ref, vT_ref, o_ref, vs, mN, pen0, sel_ref, *, tq, nb, n_sel):
    hp = pl.program_id(1)
    qi = pl.program_id(2)
    pair_w = 2 * HEAD_DIM
    row = lax.broadcasted_iota(I32, (KEY_BLOCK, tq), 0)
    lane = lax.broadcasted_iota(I32, (KEY_BLOCK, tq), 1)
    rel = lane - row

    @pl.when(qi == 0)
    def _():
        lane_m = lax.broadcasted_iota(I32, (pair_w, LANES), 1)
        means = jnp.zeros((pair_w, LANES), F32)
        for j in range(nb):
            kblk = kT_ref[0, :, j * KEY_BLOCK:(j + 1) * KEY_BLOCK]
            vs[j] = vT_ref[0, :, j * KEY_BLOCK:(j + 1) * KEY_BLOCK].astype(BF16)
            means = jnp.where(lane_m == j, jnp.mean(kblk, axis=1, keepdims=True), means)
        mN[...] = means.T.astype(BF16)
        relf = rel.astype(F32)
        for h in range(2):
            pen0[h] = sl_ref[2 * hp + h] * relf

    qT = qT_ref[0]
    gate = _mm(mN[...], _block_diag_cols(qT, 2))[:sel_ref.shape[0]]
    blk = lax.broadcasted_iota(I32, gate.shape, 0)
    sel_ref[...] = _top_select(jnp.where(blk < qi, gate, NEG_INF), n_sel, axis=0)

    q_bd = _block_diag_cols(qT * jnp.asarray(ATTN_SCALE, BF16), 2)

    def scores(j):
        k_blk = k_ref[0, pl.ds(pl.multiple_of(j * KEY_BLOCK, KEY_BLOCK), KEY_BLOCK), :]
        return _mm(k_blk, q_bd)

    s_all = scores(qi)
    v_blk = vs[qi]
    states = []
    for h in range(2):
        s = jnp.where(rel >= 0, s_all[:, h * tq:(h + 1) * tq] - pen0[h], NEG_INF)
        states.append(_online_update(_softmax_init(tq), s, 0.0, v_blk[h * HEAD_DIM:(h + 1) * HEAD_DIM]))

    def body(j, states):
        s_all = scores(j)
        v_blk = vs[j]
        off = ((qi - j) * KEY_BLOCK).astype(F32)
        out = []
        for h in range(2):
            chosen = sel_ref[pl.ds(j, 1), h * tq:(h + 1) * tq] > 0.5
            s = jnp.where(chosen, s_all[:, h * tq:(h + 1) * tq] - pen0[h], NEG_INF)
            out.append(_online_update(states[h], s, sl_ref[2 * hp + h] * off,
                                      v_blk[h * HEAD_DIM:(h + 1) * HEAD_DIM]))
        return tuple(out)

    states = lax.fori_loop(0, qi, body, tuple(states))
    oT = jnp.concatenate([acc / l for _, l, acc in states], axis=0)
    o_ref[0] = oT.T.astype(o_ref.dtype)


def _moba_prompt(qaT, ka, kaT, vaT, slopes):
    bsz, _, seq = qaT.shape
    tq = MOBA_BLOCK
    nb = seq // KEY_BLOCK
    pair_w = 2 * HEAD_DIM
    kern = functools.partial(_moba_prompt_kernel, tq=tq, nb=nb, n_sel=min(MOBA_TOPK, nb))
    kv = pl.BlockSpec((1, pair_w, seq), lambda b, hp, qi: (b, hp, 0))
    return pl.pallas_call(
        kern,
        grid=(bsz, A_HEADS // 2, seq // tq),
        in_specs=[_smem(), pl.BlockSpec((1, pair_w, tq), lambda b, hp, qi: (b, hp, qi)),
                  pl.BlockSpec((1, seq, pair_w), lambda b, hp, qi: (b, 0, hp)), kv, kv],
        out_specs=pl.BlockSpec((1, tq, pair_w), lambda b, hp, qi: (b, qi, hp)),
        out_shape=jax.ShapeDtypeStruct((bsz, seq, A_W), BF16),
        scratch_shapes=[pltpu.VMEM((nb, pair_w, KEY_BLOCK), BF16), pltpu.VMEM((LANES, pair_w), BF16),
                        pltpu.VMEM((2, KEY_BLOCK, tq), F32), pltpu.VMEM((-(-nb // 8) * 8, 2 * tq), F32)],
        compiler_params=_params(3),
        name="moba_prompt",
    )(slopes, qaT, ka, kaT, vaT)


def _dsa_prompt_kernel(sl_ref, qT_ref, iqT_ref, iwT_ref, ik_ref, k_ref, vT_ref, o_ref,
                       vbs, keys, pen0, *, tq, nb, k_sel, idx_bits):
    qi = pl.program_id(1)
    row = lax.broadcasted_iota(I32, (KEY_BLOCK, tq), 0)
    lane = lax.broadcasted_iota(I32, (KEY_BLOCK, tq), 1)
    rel = lane - row

    @pl.when(qi == 0)
    def _():
        relf = rel.astype(F32)
        for j in range(nb):
            vbs[j] = vT_ref[0, :, j * KEY_BLOCK:(j + 1) * KEY_BLOCK].astype(BF16)
        for h in range(B_HEADS):
            pen0[h] = sl_ref[h] * relf

    nkb = qi + 1

    iqT = iqT_ref[0]
    zpad = jnp.zeros((LANES - IDX_DIM, tq), BF16)
    iq_all = jnp.concatenate(
        [jnp.concatenate([iqT[h * IDX_DIM:(h + 1) * IDX_DIM], zpad], axis=0) for h in range(IDX_HEADS)], axis=1)
    w = iwT_ref[0] * (IDX_W_SCALE * IDX_SCALE)

    def idx_body(jb, carry):
        ik = ik_ref[0, pl.ds(pl.multiple_of(jb * KEY_BLOCK, KEY_BLOCK), KEY_BLOCK), :]
        r = _mm(ik, iq_all)
        sc = jnp.zeros((KEY_BLOCK, tq), F32)
        for h in range(IDX_HEADS):
            sc = sc + jnp.maximum(r[:, h * tq:(h + 1) * tq], 0.0) * w[h:h + 1, :]
        keys[jb] = jnp.where(rel + (qi - jb) * KEY_BLOCK >= 0, sc, NEG_INF)
        return carry

    lax.fori_loop(0, nkb, idx_body, 0)

    def count(pred):
        def body(jb, c):
            hit = jnp.where(pred(keys[jb], jb * KEY_BLOCK + row), 1.0, 0.0)
            return c + jnp.sum(hit.reshape(KEY_BLOCK // 8, 8, tq), axis=0)
        c = lax.fori_loop(0, nkb, body, jnp.zeros((8, tq), F32))
        return jnp.sum(c, axis=0, keepdims=True)

    def bounds():
        def body(jb, c):
            v = keys[jb].reshape(KEY_BLOCK // 8, 8, tq)
            lo = jnp.min(jnp.where(v > NEG_INF, v, jnp.inf), axis=0)
            return jnp.minimum(c[0], lo), jnp.maximum(c[1], jnp.max(v, axis=0))
        lo, hi = lax.fori_loop(0, nkb, body, (jnp.full((8, tq), jnp.inf, F32), jnp.full((8, tq), NEG_INF, F32)))
        return jnp.min(lo, axis=0, keepdims=True), jnp.max(hi, axis=0, keepdims=True)

    thr, cut = _kth_largest(count, bounds, k_sel, (1, tq), idx_bits)

    q_bd = _block_diag_cols(qT_ref[0] * jnp.asarray(ATTN_SCALE, BF16), B_HEADS)

    def att_body(jb, states):
        kk = keys[jb]
        off = (qi - jb) * KEY_BLOCK
        chosen = ((kk > thr) | ((kk == thr) & (jb * KEY_BLOCK + row <= cut))) & (rel + off >= 0)
        k_blk = k_ref[0, pl.ds(pl.multiple_of(jb * KEY_BLOCK, KEY_BLOCK), KEY_BLOCK), :]
        s_all = _mm(k_blk, q_bd)
        v_blk = vbs[jb]
        offf = off.astype(F32)
        out = []
        for h in range(B_HEADS):
            s = jnp.where(chosen, s_all[:, h * tq:(h + 1) * tq] - pen0[h], NEG_INF)
            out.append(_online_update(states[h], s, sl_ref[h] * offf, v_blk[h * HEAD_DIM:(h + 1) * HEAD_DIM]))
        return tuple(out)

    states = lax.fori_loop(0, nkb, att_body, tuple(_softmax_init(tq) for _ in range(B_HEADS)))
    oT = jnp.concatenate([acc / l for _, l, acc in states], axis=0)
    o_ref[0] = oT.T.astype(o_ref.dtype)


def _dsa_prompt(qbT, iqT, iwT, ik, kb, vbT, slopes):
    bsz, _, seq = qbT.shape
    tq = KEY_BLOCK
    nb = seq // KEY_BLOCK
    k_sel = min(DSA_TOPK, seq // 4)
    kern = functools.partial(_dsa_prompt_kernel, tq=tq, nb=nb, k_sel=k_sel, idx_bits=(seq - 1).bit_length())
    qspec = lambda w: pl.BlockSpec((1, w, tq), lambda b, qi: (b, 0, qi))
    whole = lambda a: pl.BlockSpec((1,) + a.shape[1:], lambda b, qi: (b, 0, 0))
    return pl.pallas_call(
        kern,
        grid=(bsz, seq // tq),
        in_specs=[_smem(), qspec(B_W), qspec(IDX_W), qspec(IW_ROWS), whole(ik), whole(kb), whole(vbT)],
        out_specs=pl.BlockSpec((1, tq, B_W), lambda b, qi: (b, qi, 0)),
        out_shape=jax.ShapeDtypeStruct((bsz, seq, B_W), BF16),
        scratch_shapes=[pltpu.VMEM((nb, B_W, KEY_BLOCK), BF16), pltpu.VMEM((nb, KEY_BLOCK, tq), F32),
                        pltpu.VMEM((B_HEADS, KEY_BLOCK, tq), F32)],
        compiler_params=_params(2),
        name="dsa_prompt",
    )(slopes, qbT, iqT, iwT, ik, kb, vbT)


def _mem_attn_kernel(q_ref, kT_ref, vT_ref, o_ref, *, tq):
    qs = _stack_heads(q_ref[0].astype(F32), M_HEADS, HEAD_DIM).astype(BF16)
    s = _mm(qs, kT_ref[0].astype(BF16)) * ATTN_SCALE
    p = jnp.exp(s - jnp.max(s, axis=1, keepdims=True))
    l = jnp.sum(p, axis=1, keepdims=True)
    o = _nt(p.astype(BF16), vT_ref[0].astype(BF16)) / l
    o_ref[0] = _merge_heads(o, M_HEADS, tq, HEAD_DIM).astype(o_ref.dtype)


def _mem_attn(qm, kT, vT, tq, out_dtype):
    g, t, _ = qm.shape
    n_mem = kT.shape[2]
    kv = pl.BlockSpec((1, M_W, n_mem), lambda b, i: (b, 0, 0))
    qspec = pl.BlockSpec((1, tq, M_W), lambda b, i: (b, i, 0))
    return pl.pallas_call(
        functools.partial(_mem_attn_kernel, tq=tq),
        grid=(g, t // tq),
        in_specs=[qspec, kv, kv],
        out_specs=qspec,
        out_shape=jax.ShapeDtypeStruct((g, t, M_W), out_dtype),
        compiler_params=_params(2),
        name="mem_attn",
    )(qm, kT, vT)


def _tail_kernel(x_ref, oa_ref, ob_ref, om_ref, wo_ref, g1_ref, b1_ref, w1_ref, w2_ref, g2_ref, b2_ref,
                 y_ref, *, alpha, ff_chunk):
    attn = (_mm_w(oa_ref[...], wo_ref[0:A_W, :]) + _mm_w(ob_ref[...], wo_ref[A_W:A_W + B_W, :])
            + _mm_w(om_ref[...], wo_ref[A_W + B_W:A_W + B_W + M_W, :]))
    x1 = _layer_norm(alpha * x_ref[...] + attn, g1_ref[...], b1_ref[...])
    x1w = x1.astype(w1_ref.dtype)
    d_ff = w1_ref.shape[1]
    acc = jnp.zeros(x1.shape, F32)
    for c in range(0, d_ff, ff_chunk):
        h = jnp.square(jnp.maximum(_mm_w(x1w, w1_ref[:, c:c + ff_chunk]), 0.0))
        acc = acc + _mm_w(h, w2_ref[c:c + ff_chunk, :])
    y_ref[...] = _layer_norm(alpha * x1 + acc, g2_ref[...], b2_ref[...])


def _tail(x, oa, ob, om, wo, g1, b1, w1, w2, g2, b2, alpha, tm):
    n, d = x.shape
    rows = lambda w: pl.BlockSpec((tm, w), lambda i: (i, 0))
    const = lambda a: pl.BlockSpec(a.shape, lambda i: (0, 0), pipeline_mode=pl.Buffered(1))
    return pl.pallas_call(
        functools.partial(_tail_kernel, alpha=alpha, ff_chunk=min(1024, w1.shape[1])),
        grid=(n // tm,),
        in_specs=[rows(d), rows(A_W), rows(B_W), rows(M_W), const(wo), const(g1), const(b1), const(w1),
                  const(w2), const(g2), const(b2)],
        out_specs=rows(d),
        out_shape=jax.ShapeDtypeStruct((n, d), F32),
        compiler_params=_params(1),
        name="tail",
    )(x, oa, ob, om, wo, g1, b1, w1, w2, g2, b2)


def _page_specs(n, width, layer, pp, n_groups, phase_of_use):
    def index(i):
        def f(r, ph, g, pt):
            grp = jnp.where(ph == 0, g, n_groups - 1) if phase_of_use == 0 else jnp.where(ph == 0, 0, g)
            return (layer, pt[r, grp * pp + i], 0, 0)
        return f
    return [pl.BlockSpec((None, None, width, PAGE_SIZE), index(i)) for i in range(pp)]


def _split_bf16(x):
    hi = x.astype(BF16)
    return hi, (x - hi.astype(F32)).astype(BF16)


def _moba_sample_kernel(pt_ref, sl_ref, q_ref, knT_ref, vnT_ref, *rest, pp, n_groups, n_pages, n_tok, past, n_sel):
    k_refs, v_refs = rest[:pp], rest[pp:2 * pp]
    o_ref, sc, acc_ref, l_ref = rest[2 * pp:]
    ph = pl.program_id(1)
    g = pl.program_id(2)
    m_rows = A_HEADS * n_tok
    q2 = q_ref[0]

    def scores(kT):
        k_hi, k_lo = _split_bf16(kT)
        s = _mm(q2, k_hi)
        return s[:m_rows] + s[m_rows:] + _mm(q2[:m_rows], k_lo)

    @pl.when(ph == 0)
    def _():
        for i in range(pp):
            sc[g * pp + i] = scores(k_refs[i][...])

    @pl.when((ph == 0) & (g == n_groups - 1))
    def _():
        sc[n_pages] = scores(knT_ref[0])
        lane = lax.broadcasted_iota(I32, (m_rows, LANES), 1)
        ppb = MOBA_BLOCK // PAGE_SIZE
        n_blocks = n_pages // ppb

        def gate_body(n, gate):
            tot = sc[ppb * n]
            for u in range(1, ppb):
                tot = tot + sc[ppb * n + u]
            return jnp.where(lane == n, jnp.sum(tot, axis=1, keepdims=True) * (1.0 / MOBA_BLOCK), gate)

        gate = lax.fori_loop(0, n_blocks, gate_body, jnp.full((m_rows, LANES), NEG_INF, F32),
                             unroll=math.gcd(n_blocks, 8))
        sel = _top_select(gate, n_sel).astype(BF16)

        slope3 = _slopes3(sl_ref, A_HEADS)
        lane_t = lax.broadcasted_iota(I32, (n_tok, LANES), 1)
        tok = lax.broadcasted_iota(I32, (n_tok, LANES), 0)
        blk_row = lax.broadcasted_iota(I32, (LANES, LANES), 0)

        def bias_body(n, m):
            onehot = jnp.where(blk_row == n, 1.0, 0.0).astype(BF16)
            chosen = _mm(sel, onehot) > 0.5
            for u in range(ppb):
                p = ppb * n + u
                dist = (past + tok - (p * PAGE_SIZE + lane_t)).astype(F32)
                s3 = sc[p].reshape(A_HEADS, n_tok, LANES) * ATTN_SCALE - slope3 * dist[None]
                s = jnp.where(chosen, s3.reshape(m_rows, LANES), NEG_INF)
                sc[p] = s
                m = jnp.maximum(m, s)
            return m

        m = lax.fori_loop(0, n_blocks, bias_body, jnp.full((m_rows, LANES), M_FLOOR, F32),
                          unroll=math.gcd(n_blocks, 4))
        dist = tok - lane_t
        s3 = sc[n_pages].reshape(A_HEADS, n_tok, LANES) * ATTN_SCALE - slope3 * dist.astype(F32)[None]
        s = jnp.where((dist >= 0)[None], s3, NEG_INF).reshape(m_rows, LANES)
        sc[n_pages] = s
        m = jnp.max(jnp.maximum(m, s), axis=1, keepdims=True)

        def exp_body(p, l):
            e = jnp.exp(sc[p] - m)
            sc[p] = e
            return l + e

        l = lax.fori_loop(0, n_pages + 1, exp_body, jnp.zeros((m_rows, LANES), F32))
        l_ref[...] = jnp.sum(l, axis=1, keepdims=True)

    @pl.when(ph == 1)
    def _():
        @pl.when(g == 0)
        def _():
            acc_ref[...] = jnp.zeros_like(acc_ref)
        a = acc_ref[...]
        for i in range(pp):
            a = a + _nt(sc[g * pp + i].astype(BF16), v_refs[i][...].astype(BF16))
        acc_ref[...] = a

    @pl.when((ph == 1) & (g == n_groups - 1))
    def _():
        a = acc_ref[...] + _nt(sc[n_pages].astype(BF16), vnT_ref[0].astype(BF16))
        o_ref[0] = _merge_heads(a / l_ref[...], A_HEADS, n_tok, HEAD_DIM).astype(o_ref.dtype)


def _block_diag_rows(q):
    r, t, h, dh = q.shape
    qh = jnp.transpose(q, (0, 2, 1, 3))
    eye = jnp.eye(h, dtype=bool)[None, :, None, :, None]
    out = jnp.where(eye, qh[:, :, :, None, :], 0.0)
    return out.reshape(r, h * t, h * dh)


def _new_T(x):
    xt = jnp.transpose(x, (0, 2, 1))
    return jnp.pad(xt, ((0, 0), (0, 0), (0, PAGE_SIZE - x.shape[1])))


def _moba_sample(qa, ka, va, cache_kT, cache_vT, page_table, layer, slopes):
    r, t, _ = qa.shape
    n_pages = page_table.shape[1]
    pp = min(SAMPLE_PAGES_PER_STEP, n_pages)
    n_groups = n_pages // pp
    m_rows = A_HEADS * t
    n_blocks = n_pages * PAGE_SIZE // MOBA_BLOCK
    kern = functools.partial(_moba_sample_kernel, pp=pp, n_groups=n_groups, n_pages=n_pages, n_tok=t,
                             past=n_pages * PAGE_SIZE, n_sel=min(MOBA_TOPK, n_blocks))
    per_req = lambda a: pl.BlockSpec((1,) + a.shape[1:], lambda rr, ph, g, pt: (rr, 0, 0))
    qbd = jnp.concatenate(_split_bf16(_block_diag_rows(qa.reshape(r, t, A_HEADS, HEAD_DIM))), axis=1)
    knT, vnT = _new_T(ka), _new_T(va)
    grid_spec = pltpu.PrefetchScalarGridSpec(
        num_scalar_prefetch=1,
        grid=(r, 2, n_groups),
        in_specs=[_smem(), per_req(qbd), per_req(knT), per_req(vnT)]
        + _page_specs(n_pages, A_W, layer, pp, n_groups, 0) + _page_specs(n_pages, A_W, layer, pp, n_groups, 1),
        out_specs=pl.BlockSpec((1, t, A_W), lambda rr, ph, g, pt: (rr, 0, 0)),
        scratch_shapes=[pltpu.VMEM((n_pages + 1, m_rows, LANES), F32), pltpu.VMEM((m_rows, A_W), F32),
                        pltpu.VMEM((m_rows, 1), F32)],
    )
    return pl.pallas_call(
        kern, grid_spec=grid_spec, out_shape=jax.ShapeDtypeStruct((r, t, A_W), F32),
        compiler_params=_params(3), name="moba_sample",
    )(page_table, slopes, qbd, knT, vnT, *([cache_kT] * pp), *([cache_vT] * pp))


def _dsa_sample_scores_kernel(pt_ref, q_ref, iq_ref, wb_ref, iknT_ref, knT_ref, *rest, pp, n_tok):
    i_refs, k_refs = rest[:pp], rest[pp:2 * pp]
    isc_ref, sc_ref, isc_new_ref, sc_new_ref = rest[2 * pp:]
    q = q_ref[0]
    iq2 = iq_ref[0]
    n_iq = IDX_HEADS * n_tok

    def idx_scores(ikT):
        ik_hi, ik_lo = _split_bf16(ikT)
        r = _mm(iq2, ik_hi)
        r = r[:n_iq] + r[n_iq:] + _mm(iq2[:n_iq], ik_lo)
        out = jnp.zeros((n_tok, LANES), F32)
        for h in range(IDX_HEADS):
            out = out + jnp.maximum(r[h * n_tok:(h + 1) * n_tok], 0.0) * (wb_ref[0, h] * (IDX_W_SCALE * IDX_SCALE))
        return out

    for i in range(pp):
        isc_ref[i, 0] = idx_scores(i_refs[i][...])
        sc_ref[i, 0] = _mm(q, k_refs[i][...].astype(BF16))
    lane = lax.broadcasted_iota(I32, (n_tok, LANES), 1)
    tok = lax.broadcasted_iota(I32, (n_tok, LANES), 0)
    isc_new_ref[0] = jnp.where(lane <= tok, idx_scores(iknT_ref[0]), NEG_INF)
    sc_new_ref[0] = _mm(q, knT_ref[0].astype(BF16))


def _dsa_sample_select_kernel(isc_ref, isc_new_ref, thr_ref, cut_ref, *, n_pages, k_sel, idx_bits):
    r, t, _ = isc_new_ref.shape
    lane = lax.broadcasted_iota(I32, (r, t, LANES), 2)

    def count(pred):
        def body(p, c):
            return c + jnp.where(pred(isc_ref[p], p * PAGE_SIZE + lane), 1.0, 0.0)
        c = lax.fori_loop(0, n_pages, body, jnp.zeros((r, t, LANES), F32), unroll=math.gcd(n_pages, 4))
        c = c + jnp.where(pred(isc_new_ref[...], n_pages * PAGE_SIZE + lane), 1.0, 0.0)
        return jnp.sum(c, axis=2, keepdims=True)

    def bounds():
        def body(p, c):
            v = isc_ref[p]
            return jnp.minimum(c[0], v), jnp.maximum(c[1], v)
        v = isc_new_ref[...]
        lo, hi = lax.fori_loop(0, n_pages, body, (jnp.where(v > NEG_INF, v, jnp.inf), v),
                               unroll=math.gcd(n_pages, 4))
        return jnp.min(lo, axis=2, keepdims=True), jnp.max(hi, axis=2, keepdims=True)

    thr, cut = _kth_largest(count, bounds, k_sel, (r, t, 1), idx_bits)
    thr_ref[...] = jnp.broadcast_to(thr, (r, t, LANES))
    cut_ref[...] = jnp.broadcast_to(cut, (r, t, LANES))


def _dsa_sample_attend_kernel(pt_ref, sl_ref, thr_ref, cut_ref, isc_ref, sc_ref, isc_new_ref, sc_new_ref, vnT_ref,
                              *rest, pp, n_groups, n_pages, n_tok, past):
    v_refs = rest[:pp]
    o_ref, m_ref, l_ref, acc_ref = rest[pp:]
    g = pl.program_id(1)
    m_rows = B_HEADS * n_tok
    thr, cut = thr_ref[0], cut_ref[0]
    slope3 = _slopes3(sl_ref, B_HEADS)
    lane = lax.broadcasted_iota(I32, (n_tok, LANES), 1)
    tpos = past + lax.broadcasted_iota(I32, (n_tok, LANES), 0)

    def biased(kk, raw, first):
        idx = first + lane
        chosen = ((kk > thr) | ((kk == thr) & (idx <= cut))) & (idx <= tpos)
        dist = (tpos - idx).astype(F32)
        s3 = raw.reshape(B_HEADS, n_tok, LANES) * ATTN_SCALE - slope3 * dist[None]
        return jnp.where(chosen[None], s3, NEG_INF).reshape(m_rows, LANES)

    def accumulate(tiles, values):
        top = tiles[0]
        for s in tiles[1:]:
            top = jnp.maximum(top, s)
        m_old = m_ref[...]
        m_new = jnp.maximum(m_old, jnp.max(top, axis=1, keepdims=True))
        a = jnp.exp(m_old - m_new)
        tot = jnp.zeros((m_rows, LANES), F32)
        acc = a * acc_ref[...]
        for s, v in zip(tiles, values):
            p = jnp.exp(s - m_new)
            tot = tot + p
            acc = acc + _nt(p.astype(BF16), v.astype(BF16))
        m_ref[...] = m_new
        l_ref[...] = a * l_ref[...] + jnp.sum(tot, axis=1, keepdims=True)
        acc_ref[...] = acc

    @pl.when(g == 0)
    def _():
        m_ref[...] = jnp.full(m_ref.shape, M_FLOOR, F32)
        l_ref[...] = jnp.zeros_like(l_ref)
        acc_ref[...] = jnp.zeros_like(acc_ref)
        accumulate([biased(isc_new_ref[0], sc_new_ref[0], n_pages * PAGE_SIZE)], [vnT_ref[0]])

    accumulate([biased(isc_ref[i, 0], sc_ref[i, 0], (g * pp + i) * PAGE_SIZE) for i in range(pp)],
               [v_refs[i][...] for i in range(pp)])

    @pl.when(g == n_groups - 1)
    def _():
        o_ref[0] = _merge_heads(acc_ref[...] / l_ref[...], B_HEADS, n_tok, HEAD_DIM).astype(o_ref.dtype)


def _dsa_sample(qb, kb, vb, iq, ik, iw, cache_kT, cache_vT, cache_iT, page_table, layer, slopes):
    r, t, _ = qb.shape
    n_pages = page_table.shape[1]
    past = n_pages * PAGE_SIZE
    pp = min(SAMPLE_PAGES_PER_STEP, n_pages)
    n_groups = n_pages // pp
    m_rows = B_HEADS * t
    qbd = _block_diag_rows(qb.reshape(r, t, B_HEADS, HEAD_DIM)).astype(BF16)
    iqs = jnp.transpose(iq.reshape(r, t, IDX_HEADS, IDX_DIM), (0, 2, 1, 3)).reshape(r, IDX_HEADS * t, IDX_DIM)
    iqs = jnp.concatenate(_split_bf16(iqs), axis=1)
    wbs = jnp.broadcast_to(jnp.transpose(iw, (0, 2, 1))[..., None], (r, IDX_HEADS, t, LANES))
    iknT, knT, vnT = _new_T(ik), _new_T(kb), _new_T(vb)
    per_req = lambda a: pl.BlockSpec((1,) + a.shape[1:], lambda rr, g, *_: (rr,) + (0,) * (a.ndim - 1))
    pages = lambda width: [pl.BlockSpec((None, None, width, PAGE_SIZE),
                                        functools.partial(lambda rr, g, pt, i: (layer, pt[rr, g * pp + i], 0, 0), i=i))
                           for i in range(pp)]
    paged = lambda rows: pl.BlockSpec((pp, 1, rows, LANES), lambda rr, g, *_: (g, rr, 0, 0))
    isc_shape = jax.ShapeDtypeStruct((n_pages, r, t, LANES), F32)
    sc_shape = jax.ShapeDtypeStruct((n_pages, r, m_rows, LANES), F32)
    isc_new_shape = jax.ShapeDtypeStruct((r, t, LANES), F32)
    sc_new_shape = jax.ShapeDtypeStruct((r, m_rows, LANES), F32)

    isc, sc, isc_new, sc_new = pl.pallas_call(
        functools.partial(_dsa_sample_scores_kernel, pp=pp, n_tok=t),
        grid_spec=pltpu.PrefetchScalarGridSpec(
            num_scalar_prefetch=1, grid=(r, n_groups),
            in_specs=[per_req(qbd), per_req(iqs), per_req(wbs), per_req(iknT), per_req(knT)]
            + pages(IDX_DIM) + pages(B_W),
            out_specs=[paged(t), paged(m_rows), per_req(isc_new_shape), per_req(sc_new_shape)]),
        out_shape=[isc_shape, sc_shape, isc_new_shape, sc_new_shape],
        compiler_params=_params(2), name="dsa_sample_scores",
    )(page_table, qbd, iqs, wbs, iknT, knT, *([cache_iT] * pp), *([cache_kT] * pp))

    whole = lambda a: pl.BlockSpec(a.shape, lambda i: (0,) * a.ndim, pipeline_mode=pl.Buffered(1))
    thr, cut = pl.pallas_call(
        functools.partial(_dsa_sample_select_kernel, n_pages=n_pages, k_sel=min(DSA_TOPK, (past + t) // 4),
                          idx_bits=(past + PAGE_SIZE - 1).bit_length()),
        grid=(1,),
        in_specs=[whole(isc_shape), whole(isc_new_shape)],
        out_specs=[pl.BlockSpec((r, t, LANES), lambda i: (0, 0, 0))] * 2,
        out_shape=[isc_new_shape, jax.ShapeDtypeStruct((r, t, LANES), I32)],
        compiler_params=_params(1), name="dsa_sample_select",
    )(isc, isc_new)

    return pl.pallas_call(
        functools.partial(_dsa_sample_attend_kernel, pp=pp, n_groups=n_groups, n_pages=n_pages, n_tok=t, past=past),
        grid_spec=pltpu.PrefetchScalarGridSpec(
            num_scalar_prefetch=1, grid=(r, n_groups),
            in_specs=[_smem(), per_req(thr), per_req(cut), paged(t), paged(m_rows), per_req(isc_new),
                      per_req(sc_new), per_req(vnT)] + pages(B_W),
            out_specs=pl.BlockSpec((1, t, B_W), lambda rr, g, *_: (rr, 0, 0)),
            scratch_shapes=[pltpu.VMEM((m_rows, 1), F32), pltpu.VMEM((m_rows, 1), F32),
                            pltpu.VMEM((m_rows, B_W), F32)]),
        out_shape=jax.ShapeDtypeStruct((r, t, B_W), F32),
        compiler_params=_params(2), name="dsa_sample_attend",
    )(page_table, slopes, thr, cut, isc, sc, isc_new, sc_new, vnT, *([cache_vT] * pp))


def _alibi_slopes(n):
    return jnp.exp2(-8.0 * jnp.arange(1, n + 1, dtype=F32) / n)


def _heads_T(xT, heads):
    b, _, s = xT.shape
    return jnp.transpose(xT.reshape(b, heads, HEAD_DIM, s), (0, 3, 1, 2))


def _paged_T(cache):
    if cache.ndim == 5:
        d, n, p, h, dh = cache.shape
        return jnp.transpose(cache, (0, 1, 3, 4, 2)).reshape(d, n, h * dh, p)
    return jnp.transpose(cache, (0, 1, 3, 2))


def kernel(x_prompt, x_sample, cache_a_k, cache_a_v, cache_b_k, cache_b_v, cache_idx_k, cache_mem_k, cache_mem_v,
           page_table, mem_prompt, w_in, w_mem_kv, w_out, ln1_g, ln1_b, w_ff1, w_ff2, ln2_g, ln2_b):
    depth = w_in.shape[0]
    bsz, seq, d_model = x_prompt.shape
    r, t, _ = x_sample.shape
    assert seq % KEY_BLOCK == 0 and page_table.shape[1] * PAGE_SIZE % MOBA_BLOCK == 0
    assert page_table.shape[1] * PAGE_SIZE // MOBA_BLOCK <= LANES and seq // MOBA_BLOCK <= LANES
    alpha = (2 * depth) ** 0.25
    slopes_a, slopes_b = _alibi_slopes(A_HEADS), _alibi_slopes(B_HEADS)
    tm_proj = min(512, seq)
    tm_tail = min(512, bsz * seq)

    vec = lambda a, l: a[l].reshape(1, d_model)

    def tail(x2d, oa, ob, om, l, tm, wdtype):
        n = x2d.shape[0]
        return _tail(x2d, oa.reshape(n, A_W), ob.reshape(n, B_W), om.reshape(n, M_W), w_out[l].astype(wdtype),
                     vec(ln1_g, l), vec(ln1_b, l), w_ff1[l].astype(wdtype), w_ff2[l].astype(wdtype),
                     vec(ln2_g, l), vec(ln2_b, l), alpha, tm)

    x = x_prompt
    p_states = []
    for l in range(depth):
        wn, wt = _split_w_in(w_in[l])
        ka, kb, qm, ik, kaT, vaT, kbT, vbT, ikT, qaT, qbT, iqT, iwT = _proj_prompt(x, wn, wt, tm_proj)
        kvT = _mem_kv(mem_prompt, w_mem_kv[l].T.astype(BF16))
        kmT, vmT = kvT[:, :M_W], kvT[:, M_W:]
        o_a = _moba_prompt(qaT, ka, kaT, vaT, slopes_a)
        o_b = _dsa_prompt(qbT, iqT, iwT, ik, kb, vbT, slopes_b)
        o_m = _mem_attn(qm, kmT, vmT, KEY_BLOCK, BF16)
        x = tail(x.reshape(bsz * seq, d_model), o_a, o_b, o_m, l, tm_tail, BF16).reshape(bsz, seq, d_model)
        p_states.append((_heads_T(kaT, A_HEADS), _heads_T(vaT, A_HEADS), _heads_T(kbT, B_HEADS),
                         _heads_T(vbT, B_HEADS), jnp.transpose(ikT, (0, 2, 1)),
                         _heads_T(kmT, M_HEADS), _heads_T(vmT, M_HEADS)))
    y_prompt = x

    caT, cvT = _paged_T(cache_a_k), _paged_T(cache_a_v)
    cbkT, cbvT, ciT = _paged_T(cache_b_k), _paged_T(cache_b_v), _paged_T(cache_idx_k)
    n_mem = cache_mem_k.shape[2]
    mem_T = lambda c: jnp.transpose(c, (0, 2, 3, 1)).reshape(r, M_W, n_mem)
    o = _OFF
    x = x_sample.reshape(r * t, d_model)
    s_states = []
    for l in range(depth):
        p = _matmul(x, w_in[l]).reshape(r, t, -1)
        qa, ka, va, qb, kb, vb, iq, ik, iw, qm = [p[..., o[i]:o[i + 1]] for i in range(10)]
        o_a = _moba_sample(qa, ka, va, caT, cvT, page_table, l, slopes_a)
        o_b = _dsa_sample(qb, kb, vb, iq, ik, iw, cbkT, cbvT, ciT, page_table, l, slopes_b)
        o_m = _mem_attn(qm, mem_T(cache_mem_k[l]), mem_T(cache_mem_v[l]), t, F32)
        x = tail(x, o_a, o_b, o_m, l, r * t, F32)
        s_states.append((ka.reshape(r, t, A_HEADS, HEAD_DIM), va.reshape(r, t, A_HEADS, HEAD_DIM),
                         kb.reshape(r, t, B_HEADS, HEAD_DIM), vb.reshape(r, t, B_HEADS, HEAD_DIM), ik))
    y_sample = x.reshape(r, t, d_model)

    p_out = [jnp.stack(z, axis=0) for z in zip(*p_states)]
    s_out = [jnp.stack(z, axis=0) for z in zip(*s_states)]
    return (y_prompt, y_sample, *p_out, *s_out)
```

```python
import functools
import math

import jax
import jax.numpy as jnp
from jax import lax
from jax.experimental import pallas as pl
from jax.experimental.pallas import tpu as pltpu

F32, BF16, I32 = jnp.float32, jnp.bfloat16, jnp.int32

HEAD_DIM = 64
A_HEADS, B_HEADS, M_HEADS = 8, 4, 4
IDX_HEADS, IDX_DIM = 8, 32
A_W, B_W, M_W = A_HEADS * HEAD_DIM, B_HEADS * HEAD_DIM, M_HEADS * HEAD_DIM
IDX_W = IDX_HEADS * IDX_DIM
MOBA_BLOCK, MOBA_TOPK, DSA_TOPK, PAGE_SIZE = 256, 3, 256, 128
LN_EPS = 1e-5
ATTN_SCALE = HEAD_DIM ** -0.5
IDX_SCALE = IDX_DIM ** -0.5
IDX_W_SCALE = IDX_HEADS ** -0.5
_SPLITS = (A_W, A_W, A_W, B_W, B_W, B_W, IDX_W, IDX_DIM, IDX_HEADS, M_W)
_OFF = tuple(sum(_SPLITS[:i]) for i in range(len(_SPLITS) + 1))

LANES = 128
KEY_BLOCK = 256
ROW_CHUNK = 256
NEG_INF = float("-inf")
M_FLOOR = -1e30
ALL_TIES = 1e9
VMEM_LIMIT = 56 * 1024 * 1024
SAMPLE_PAGES_PER_STEP = 32
VALUE_PASSES = 26


def _mm(a, b):
    return jnp.dot(a, b, preferred_element_type=F32)


def _nt(a, b):
    return lax.dot_general(a, b, (((1,), (1,)), ((), ())), preferred_element_type=F32)


def _params(n_grid_dims):
    return pltpu.CompilerParams(dimension_semantics=("arbitrary",) * n_grid_dims,
                                vmem_limit_bytes=VMEM_LIMIT)


def _smem():
    return pl.BlockSpec(memory_space=pltpu.SMEM)


def _slopes3(sl_ref, n):
    i = lax.broadcasted_iota(I32, (n, 1, 1), 0)
    out = jnp.full((n, 1, 1), sl_ref[0], F32)
    for h in range(1, n):
        out = jnp.where(i == h, sl_ref[h], out)
    return out


def _stack_heads(q2, n_heads, width):
    lane = lax.broadcasted_iota(I32, q2.shape, 1)
    zero = jnp.zeros_like(q2)
    return jnp.concatenate(
        [jnp.where((lane >= h * width) & (lane < (h + 1) * width), q2, zero) for h in range(n_heads)], axis=0)


def _merge_heads(x, n_heads, t, width):
    lane = lax.broadcasted_iota(I32, (t, n_heads * width), 1)
    out = x[0:t]
    for h in range(1, n_heads):
        out = jnp.where(lane >= h * width, x[h * t:(h + 1) * t], out)
    return out


def _top_select(gate, n_sel, axis=1):
    blk = lax.broadcasted_iota(I32, gate.shape, axis)
    sel = jnp.zeros(gate.shape, F32)
    g = gate
    for _ in range(n_sel):
        m = jnp.max(g, axis=axis, keepdims=True)
        first = jnp.min(jnp.where(g == m, blk, jnp.int32(1 << 30)), axis=axis, keepdims=True)
        pick = (blk == first) & (m > NEG_INF)
        sel = jnp.where(pick, 1.0, sel)
        g = jnp.where(pick, NEG_INF, g)
    return sel


def _key_to_f32(key):
    bits = jnp.where(key < 0, key ^ jnp.int32(0x7FFFFFFF), key)
    return lax.bitcast_convert_type(bits, F32)


def _kth_largest(count, bounds, k, shape):
    kf = jnp.float32(k)
    int_min = jnp.int32(-2 ** 31)

    def settle(cf, cnt, done, exact):
        hit = (cnt == kf) & (done < 0.5)
        return jnp.where(hit, 1.0, done), jnp.where(hit, cf, exact)

    ge0 = count(lambda v: v >= 0.0)
    gt0 = count(lambda v: v > 0.0)
    zero_tie = (gt0 < kf) & (ge0 >= kf)
    need0 = jnp.where(zero_tie, kf - gt0, ALL_TIES)
    zeros = jnp.zeros(shape, F32)

    lo, hi = bounds()

    def acond(c):
        return (c[0] < VALUE_PASSES // 2) & (c[1] > 0.0)

    def abody(c):
        g, _, lo, hi, done, exact = c
        for _ in range(2):
            mid = 0.5 * lo + 0.5 * hi
            cnt = count(lambda v: v >= mid)
            done, exact = settle(mid, cnt, done, exact)
            lo = jnp.where(cnt > kf, mid, lo)
            hi = jnp.where(cnt < kf, mid, hi)
        return g + 1, jnp.sum(1.0 - done), lo, hi, done, exact

    done0 = jnp.where(zero_tie, 1.0, 0.0)
    _, n_open, _, _, done, exact = lax.while_loop(
        acond, abody, (jnp.int32(0), jnp.sum(1.0 - done0), lo, hi, done0, zeros))

    def bitwise(done, exact):
        key = jnp.where(ge0 >= kf, jnp.zeros(shape, I32), jnp.full(shape, int_min, I32))

        def vcond(c):
            return (c[0] < 31) & (c[1] > 0.0)

        def vbody(c):
            b, _, key, done, exact = c
            cand = key | lax.shift_left(jnp.int32(1), jnp.int32(30) - b)
            cf = _key_to_f32(cand)
            cnt = count(lambda v: v >= cf)
            key = jnp.where(cnt >= kf, cand, key)
            done, exact = settle(cf, cnt, done, exact)
            return b + 1, jnp.sum(1.0 - done), key, done, exact

        _, _, key, done_b, exact = lax.while_loop(
            vcond, vbody, (jnp.int32(0), jnp.sum(1.0 - done), key, done, exact))
        thr = _key_to_f32(key)
        thr = jnp.where(thr != thr, NEG_INF, thr)
        thr = jnp.where(done_b > 0.5, exact, thr)
        need = jnp.where(done_b > 0.5, need0, kf - count(lambda v: v > thr))
        return thr, need

    return lax.cond(n_open > 0.0, bitwise, lambda done, exact: (exact, need0), done, exact)


def _index_cut(count_eq_below, need, shape, idx_bits):
    def ibody(b, cut):
        cand = cut | lax.shift_left(jnp.int32(1), jnp.int32(idx_bits - 1) - b)
        return jnp.where(count_eq_below(cand) < need, cand, cut)

    return lax.fori_loop(0, idx_bits, ibody, jnp.zeros(shape, I32))


def _layer_norm(x, g, b):
    mu = jnp.mean(x, axis=-1, keepdims=True)
    xc = x - mu
    var = jnp.mean(xc * xc, axis=-1, keepdims=True)
    return xc * lax.rsqrt(var + LN_EPS) * g + b


IW_ROWS = 16
_NAT = ((A_W, "k_a"), (B_W, "k_b"), (M_W, "q_mem"), (LANES, "k_idx"))
_TR = ((A_W, F32), (A_W, F32), (B_W, F32), (B_W, F32), (IDX_DIM, F32),
       (A_W, BF16), (B_W, BF16), (IDX_W, BF16), (IW_ROWS, F32))


def _split_w_in(w):
    o = _OFF
    d = w.shape[0]
    wn = jnp.concatenate([w[:, o[1]:o[2]], w[:, o[4]:o[5]], w[:, o[9]:o[10]], w[:, o[7]:o[8]],
                          jnp.zeros((d, LANES - IDX_DIM), w.dtype)], axis=1)
    wt = jnp.concatenate([w[:, o[1]:o[2]], w[:, o[2]:o[3]], w[:, o[4]:o[5]], w[:, o[5]:o[6]], w[:, o[7]:o[8]],
                          w[:, o[0]:o[1]], w[:, o[3]:o[4]], w[:, o[6]:o[7]], w[:, o[8]:o[9]],
                          jnp.zeros((d, IW_ROWS - IDX_HEADS), w.dtype)], axis=1).T
    return wn.astype(BF16), wt.astype(BF16)


def _proj_prompt_kernel(x_ref, wn_ref, wt_ref, *out_refs):
    xb = x_ref[0].astype(BF16)
    nat = _mm(xb, wn_ref[...])
    c = 0
    for ref, (w, _) in zip(out_refs[:len(_NAT)], _NAT):
        ref[0] = nat[:, c:c + w].astype(ref.dtype)
        c += w
    tr = _nt(wt_ref[...], xb)
    c = 0
    for ref, (w, _) in zip(out_refs[len(_NAT):], _TR):
        ref[0] = tr[c:c + w].astype(ref.dtype)
        c += w


def _proj_prompt(x, wn, wt, tm):
    bsz, seq, d = x.shape
    nat = lambda w: pl.BlockSpec((1, tm, w), lambda b, i: (b, i, 0))
    tr = lambda w: pl.BlockSpec((1, w, tm), lambda b, i: (b, 0, i))
    full = lambda a: pl.BlockSpec(a.shape, lambda b, i: (0, 0))
    return pl.pallas_call(
        _proj_prompt_kernel,
        grid=(bsz, seq // tm),
        in_specs=[nat(d), full(wn), full(wt)],
        out_specs=[nat(w) for w, _ in _NAT] + [tr(w) for w, _ in _TR],
        out_shape=[jax.ShapeDtypeStruct((bsz, seq, w), BF16) for w, _ in _NAT]
        + [jax.ShapeDtypeStruct((bsz, w, seq), dt) for w, dt in _TR],
        compiler_params=_params(2),
        name="proj_prompt",
    )(x, wn, wt)


def _mm_w(a, w):
    if w.dtype == F32:
        return jnp.dot(a.astype(F32), w, preferred_element_type=F32, precision=lax.Precision.HIGHEST)
    return _mm(a.astype(BF16), w)


def _matmul_kernel(x_ref, w_ref, o_ref):
    o_ref[...] = _mm_w(x_ref[...], w_ref[...])


def _matmul(x, w):
    m, n = x.shape[0], w.shape[1]
    return pl.pallas_call(
        _matmul_kernel,
        grid=(1,),
        in_specs=[pl.BlockSpec(x.shape, lambda i: (0, 0)), pl.BlockSpec(w.shape, lambda i: (0, 0))],
        out_specs=pl.BlockSpec((m, n), lambda i: (0, 0)),
        out_shape=jax.ShapeDtypeStruct((m, n), F32),
        compiler_params=_params(1),
        name="proj_sample",
    )(x, w)


def _mem_kv_kernel(wT_ref, m_ref, o_ref):
    o_ref[0] = _nt(wT_ref[...], m_ref[0].astype(BF16))


def _mem_kv(mem, wT):
    bsz, n_mem, d = mem.shape
    return pl.pallas_call(
        _mem_kv_kernel,
        grid=(bsz,),
        in_specs=[pl.BlockSpec(wT.shape, lambda b: (0, 0)), pl.BlockSpec((1, n_mem, d), lambda b: (b, 0, 0))],
        out_specs=pl.BlockSpec((1, wT.shape[0], n_mem), lambda b: (b, 0, 0)),
        out_shape=jax.ShapeDtypeStruct((bsz, wT.shape[0], n_mem), F32),
        compiler_params=_params(1),
        name="mem_kv",
    )(wT, mem)


def _block_diag_cols(qT, n_heads):
    hrow = lax.broadcasted_iota(I32, qT.shape, 0) // HEAD_DIM
    zero = jnp.zeros_like(qT)
    return jnp.concatenate([jnp.where(hrow == h, qT, zero) for h in range(n_heads)], axis=1)


def _online_update(state, s, shift0, v_rows):
    m, l, acc = state
    m_new = jnp.maximum(m, jnp.max(s, axis=0, keepdims=True) - shift0)
    a = jnp.exp(m - m_new)
    p = jnp.exp(s - (m_new + shift0))
    return m_new, a * l + jnp.sum(p, axis=0, keepdims=True), a * acc + _mm(v_rows, p.astype(BF16))


def _softmax_init(t):
    return (jnp.full((1, t), M_FLOOR, F32), jnp.zeros((1, t), F32), jnp.zeros((HEAD_DIM, t), F32))


def _moba_prompt_kernel(sl_ref, qT_ref, k_ref, kT_ref, vT_ref, o_ref, vs, mN, pen0, sel_ref, *, tq, nb, n_sel):
    hp = pl.program_id(1)
    qi = pl.program_id(2)
    pair_w = 2 * HEAD_DIM
    row = lax.broadcasted_iota(I32, (KEY_BLOCK, tq), 0)
    lane = lax.broadcasted_iota(I32, (KEY_BLOCK, tq), 1)
    rel = lane - row

    @pl.when(qi == 0)
    def _():
        lane_m = lax.broadcasted_iota(I32, (pair_w, LANES), 1)
        means = jnp.zeros((pair_w, LANES), F32)
        for j in range(nb):
            kblk = kT_ref[0, :, j * KEY_BLOCK:(j + 1) * KEY_BLOCK]
            vs[j] = vT_ref[0, :, j * KEY_BLOCK:(j + 1) * KEY_BLOCK].astype(BF16)
            means = jnp.where(lane_m == j, jnp.mean(kblk, axis=1, keepdims=True), means)
        mN[...] = means.T.astype(BF16)
        relf = rel.astype(F32)
        for h in range(2):
            pen0[h] = sl_ref[2 * hp + h] * relf

    qT = qT_ref[0]
    gate = _mm(mN[...], _block_diag_cols(qT, 2))[:sel_ref.shape[0]]
    blk = lax.broadcasted_iota(I32, gate.shape, 0)
    sel_ref[...] = _top_select(jnp.where(blk < qi, gate, NEG_INF), n_sel, axis=0)

    q_bd = _block_diag_cols(qT * jnp.asarray(ATTN_SCALE, BF16), 2)

    def scores(j):
        k_blk = k_ref[0, pl.ds(pl.multiple_of(j * KEY_BLOCK, KEY_BLOCK), KEY_BLOCK), :]
        return _mm(k_blk, q_bd)

    s_all = scores(qi)
    v_blk = vs[qi]
    states = []
    for h in range(2):
        s = jnp.where(rel >= 0, s_all[:, h * tq:(h + 1) * tq] - pen0[h], NEG_INF)
        states.append(_online_update(_softmax_init(tq), s, 0.0, v_blk[h * HEAD_DIM:(h + 1) * HEAD_DIM]))

    def body(j, states):
        s_all = scores(j)
        v_blk = vs[j]
        off = ((qi - j) * KEY_BLOCK).astype(F32)
        out = []
        for h in range(2):
            chosen = sel_ref[pl.ds(j, 1), h * tq:(h + 1) * tq] > 0.5
            s = jnp.where(chosen, s_all[:, h * tq:(h + 1) * tq] - pen0[h], NEG_INF)
            out.append(_online_update(states[h], s, sl_ref[2 * hp + h] * off,
                                      v_blk[h * HEAD_DIM:(h + 1) * HEAD_DIM]))
        return tuple(out)

    states = lax.fori_loop(0, qi, body, tuple(states))
    oT = jnp.concatenate([acc / l for _, l, acc in states], axis=0)
    o_ref[0] = oT.T.astype(o_ref.dtype)


def _moba_prompt(qaT, ka, kaT, vaT, slopes):
    bsz, _, seq = qaT.shape
    tq = MOBA_BLOCK
    nb = seq // KEY_BLOCK
    pair_w = 2 * HEAD_DIM
    kern = functools.partial(_moba_prompt_kernel, tq=tq, nb=nb, n_sel=min(MOBA_TOPK, nb))
    kv = pl.BlockSpec((1, pair_w, seq), lambda b, hp, qi: (b, hp, 0))
    return pl.pallas_call(
        kern,
        grid=(bsz, A_HEADS // 2, seq // tq),
        in_specs=[_smem(), pl.BlockSpec((1, pair_w, tq), lambda b, hp, qi: (b, hp, qi)),
                  pl.BlockSpec((1, seq, pair_w), lambda b, hp, qi: (b, 0, hp)), kv, kv],
        out_specs=pl.BlockSpec((1, tq, pair_w), lambda b, hp, qi: (b, qi, hp)),
        out_shape=jax.ShapeDtypeStruct((bsz, seq, A_W), BF16),
        scratch_shapes=[pltpu.VMEM((nb, pair_w, KEY_BLOCK), BF16), pltpu.VMEM((LANES, pair_w), BF16),
                        pltpu.VMEM((2, KEY_BLOCK, tq), F32), pltpu.VMEM((-(-nb // 8) * 8, 2 * tq), F32)],
        compiler_params=_params(3),
        name="moba_prompt",
    )(slopes, qaT, ka, kaT, vaT)


def _dsa_prompt_kernel(sl_ref, qT_ref, iqT_ref, iwT_ref, ik_ref, k_ref, vT_ref, o_ref,
                       vbs, keys, pen0, *, tq, nb, k_sel):
    qi = pl.program_id(1)
    row = lax.broadcasted_iota(I32, (KEY_BLOCK, tq), 0)
    lane = lax.broadcasted_iota(I32, (KEY_BLOCK, tq), 1)
    rel = lane - row

    @pl.when(qi == 0)
    def _():
        relf = rel.astype(F32)
        for j in range(nb):
            vbs[j] = vT_ref[0, :, j * KEY_BLOCK:(j + 1) * KEY_BLOCK].astype(BF16)
        for h in range(B_HEADS):
            pen0[h] = sl_ref[h] * relf

    nkb = qi + 1

    iqT = iqT_ref[0]
    zpad = jnp.zeros((LANES - IDX_DIM, tq), BF16)
    iq_all = jnp.concatenate(
        [jnp.concatenate([iqT[h * IDX_DIM:(h + 1) * IDX_DIM], zpad], axis=0) for h in range(IDX_HEADS)], axis=1)
    w = iwT_ref[0] * (IDX_W_SCALE * IDX_SCALE)

    def idx_body(jb, carry):
        ik = ik_ref[0, pl.ds(pl.multiple_of(jb * KEY_BLOCK, KEY_BLOCK), KEY_BLOCK), :]
        r = _mm(ik, iq_all)
        sc = jnp.zeros((KEY_BLOCK, tq), F32)
        for h in range(IDX_HEADS):
            sc = sc + jnp.maximum(r[:, h * tq:(h + 1) * tq], 0.0) * w[h:h + 1, :]
        keys[jb] = jnp.where(rel + (qi - jb) * KEY_BLOCK >= 0, sc, NEG_INF)
        return carry

    lax.fori_loop(0, nkb, idx_body, 0)

    def count(pred):
        def body(jb, c):
            hit = jnp.where(pred(keys[jb]), 1.0, 0.0)
            return c + jnp.sum(hit.reshape(KEY_BLOCK // 8, 8, tq), axis=0)
        c = lax.fori_loop(0, nkb, body, jnp.zeros((8, tq), F32))
        return jnp.sum(c, axis=0, keepdims=True)

    def bounds():
        def body(jb, c):
            v = keys[jb].reshape(KEY_BLOCK // 8, 8, tq)
            lo = jnp.min(jnp.where(v > NEG_INF, v, jnp.inf), axis=0)
            return jnp.minimum(c[0], lo), jnp.maximum(c[1], jnp.max(v, axis=0))
        lo, hi = lax.fori_loop(0, nkb, body, (jnp.full((8, tq), jnp.inf, F32), jnp.full((8, tq), NEG_INF, F32)))
        return jnp.min(lo, axis=0, keepdims=True), jnp.max(hi, axis=0, keepdims=True)

    thr, need = _kth_largest(count, bounds, k_sel, (1, tq))

    q_bd = _block_diag_cols(qT_ref[0] * jnp.asarray(ATTN_SCALE, BF16), B_HEADS)
    lower_tri = jnp.where(lax.broadcasted_iota(I32, (KEY_BLOCK, KEY_BLOCK), 0)
                          >= lax.broadcasted_iota(I32, (KEY_BLOCK, KEY_BLOCK), 1), 1.0, 0.0).astype(BF16)

    def att_body(jb, carry):
        seen, states = carry
        kk = keys[jb]
        off = (qi - jb) * KEY_BLOCK
        tie = kk == thr
        rank = _mm(lower_tri, jnp.where(tie, 1.0, 0.0).astype(BF16)) + seen
        chosen = ((kk > thr) | (tie & (rank <= need))) & (rel + off >= 0)
        k_blk = k_ref[0, pl.ds(pl.multiple_of(jb * KEY_BLOCK, KEY_BLOCK), KEY_BLOCK), :]
        s_all = _mm(k_blk, q_bd)
        v_blk = vbs[jb]
        offf = off.astype(F32)
        out = []
        for h in range(B_HEADS):
            s = jnp.where(chosen, s_all[:, h * tq:(h + 1) * tq] - pen0[h], NEG_INF)
            out.append(_online_update(states[h], s, sl_ref[h] * offf, v_blk[h * HEAD_DIM:(h + 1) * HEAD_DIM]))
        return rank[KEY_BLOCK - 1:KEY_BLOCK], tuple(out)

    _, states = lax.fori_loop(0, nkb, att_body,
                              (jnp.zeros((1, tq), F32), tuple(_softmax_init(tq) for _ in range(B_HEADS))))
    oT = jnp.concatenate([acc / l for _, l, acc in states], axis=0)
    o_ref[0] = oT.T.astype(o_ref.dtype)


def _dsa_prompt(qbT, iqT, iwT, ik, kb, vbT, slopes):
    bsz, _, seq = qbT.shape
    tq = KEY_BLOCK
    nb = seq // KEY_BLOCK
    k_sel = min(DSA_TOPK, seq // 4)
    kern = functools.partial(_dsa_prompt_kernel, tq=tq, nb=nb, k_sel=k_sel)
    qspec = lambda w: pl.BlockSpec((1, w, tq), lambda b, qi: (b, 0, qi))
    whole = lambda a: pl.BlockSpec((1,) + a.shape[1:], lambda b, qi: (b, 0, 0))
    return pl.pallas_call(
        kern,
        grid=(bsz, seq // tq),
        in_specs=[_smem(), qspec(B_W), qspec(IDX_W), qspec(IW_ROWS), whole(ik), whole(kb), whole(vbT)],
        out_specs=pl.BlockSpec((1, tq, B_W), lambda b, qi: (b, qi, 0)),
        out_shape=jax.ShapeDtypeStruct((bsz, seq, B_W), BF16),
        scratch_shapes=[pltpu.VMEM((nb, B_W, KEY_BLOCK), BF16), pltpu.VMEM((nb, KEY_BLOCK, tq), F32),
                        pltpu.VMEM((B_HEADS, KEY_BLOCK, tq), F32)],
        compiler_params=_params(2),
        name="dsa_prompt",
    )(slopes, qbT, iqT, iwT, ik, kb, vbT)


def _mem_attn_kernel(q_ref, kT_ref, vT_ref, o_ref, *, tq):
    qs = _stack_heads(q_ref[0].astype(F32), M_HEADS, HEAD_DIM).astype(BF16)
    s = _mm(qs, kT_ref[0].astype(BF16)) * ATTN_SCALE
    p = jnp.exp(s - jnp.max(s, axis=1, keepdims=True))
    l = jnp.sum(p, axis=1, keepdims=True)
    o = _nt(p.astype(BF16), vT_ref[0].astype(BF16)) / l
    o_ref[0] = _merge_heads(o, M_HEADS, tq, HEAD_DIM).astype(o_ref.dtype)


def _mem_attn(qm, kT, vT, tq, out_dtype):
    g, t, _ = qm.shape
    n_mem = kT.shape[2]
    kv = pl.BlockSpec((1, M_W, n_mem), lambda b, i: (b, 0, 0))
    qspec = pl.BlockSpec((1, tq, M_W), lambda b, i: (b, i, 0))
    return pl.pallas_call(
        functools.partial(_mem_attn_kernel, tq=tq),
        grid=(g, t // tq),
        in_specs=[qspec, kv, kv],
        out_specs=qspec,
        out_shape=jax.ShapeDtypeStruct((g, t, M_W), out_dtype),
        compiler_params=_params(2),
        name="mem_attn",
    )(qm, kT, vT)


def _tail_kernel(x_ref, oa_ref, ob_ref, om_ref, wo_ref, g1_ref, b1_ref, w1_ref, w2_ref, g2_ref, b2_ref,
                 y_ref, *, alpha, ff_chunk):
    attn = (_mm_w(oa_ref[...], wo_ref[0:A_W, :]) + _mm_w(ob_ref[...], wo_ref[A_W:A_W + B_W, :])
            + _mm_w(om_ref[...], wo_ref[A_W + B_W:A_W + B_W + M_W, :]))
    x1 = _layer_norm(alpha * x_ref[...] + attn, g1_ref[...], b1_ref[...])
    x1w = x1.astype(w1_ref.dtype)
    d_ff = w1_ref.shape[1]
    acc = jnp.zeros(x1.shape, F32)
    for c in range(0, d_ff, ff_chunk):
        h = jnp.square(jnp.maximum(_mm_w(x1w, w1_ref[:, c:c + ff_chunk]), 0.0))
        acc = acc + _mm_w(h, w2_ref[c:c + ff_chunk, :])
    y_ref[...] = _layer_norm(alpha * x1 + acc, g2_ref[...], b2_ref[...])


def _tail(x, oa, ob, om, wo, g1, b1, w1, w2, g2, b2, alpha, tm):
    n, d = x.shape
    rows = lambda w: pl.BlockSpec((tm, w), lambda i: (i, 0))
    const = lambda a: pl.BlockSpec(a.shape, lambda i: (0, 0), pipeline_mode=pl.Buffered(1))
    return pl.pallas_call(
        functools.partial(_tail_kernel, alpha=alpha, ff_chunk=min(1024, w1.shape[1])),
        grid=(n // tm,),
        in_specs=[rows(d), rows(A_W), rows(B_W), rows(M_W), const(wo), const(g1), const(b1), const(w1),
                  const(w2), const(g2), const(b2)],
        out_specs=rows(d),
        out_shape=jax.ShapeDtypeStruct((n, d), F32),
        compiler_params=_params(1),
        name="tail",
    )(x, oa, ob, om, wo, g1, b1, w1, w2, g2, b2)


def _page_specs(n, width, layer, pp, n_groups, phase_of_use):
    def index(i):
        def f(r, ph, g, pt):
            grp = jnp.where(ph == 0, g, n_groups - 1) if phase_of_use == 0 else jnp.where(ph == 0, 0, g)
            return (layer, pt[r, grp * pp + i], 0, 0)
        return f
    return [pl.BlockSpec((None, None, width, PAGE_SIZE), index(i)) for i in range(pp)]


def _split_bf16(x):
    hi = x.astype(BF16)
    return hi, (x - hi.astype(F32)).astype(BF16)


def _moba_sample_kernel(pt_ref, sl_ref, q_ref, knT_ref, vnT_ref, *rest, pp, n_groups, n_pages, n_tok, past, n_sel):
    k_refs, v_refs = rest[:pp], rest[pp:2 * pp]
    o_ref, sc, acc_ref, l_ref = rest[2 * pp:]
    ph = pl.program_id(1)
    g = pl.program_id(2)
    m_rows = A_HEADS * n_tok
    q2 = jnp.concatenate(_split_bf16(q_ref[0]), axis=0)

    def scores(kT):
        k_hi, k_lo = _split_bf16(kT)
        s = _mm(q2, k_hi)
        return s[:m_rows] + s[m_rows:] + _mm(q2[:m_rows], k_lo)

    @pl.when(ph == 0)
    def _():
        for i in range(pp):
            sc[g * pp + i] = scores(k_refs[i][...])

    @pl.when((ph == 0) & (g == n_groups - 1))
    def _():
        sc[n_pages] = scores(knT_ref[0])
        lane = lax.broadcasted_iota(I32, (m_rows, LANES), 1)
        ppb = MOBA_BLOCK // PAGE_SIZE
        n_blocks = n_pages // ppb

        def gate_body(n, gate):
            tot = sc[ppb * n]
            for u in range(1, ppb):
                tot = tot + sc[ppb * n + u]
            return jnp.where(lane == n, jnp.sum(tot, axis=1, keepdims=True) * (1.0 / MOBA_BLOCK), gate)

        gate = lax.fori_loop(0, n_blocks, gate_body, jnp.full((m_rows, LANES), NEG_INF, F32),
                             unroll=math.gcd(n_blocks, 8))
        sel = _top_select(gate, n_sel).astype(BF16)

        slope3 = _slopes3(sl_ref, A_HEADS)
        lane_t = lax.broadcasted_iota(I32, (n_tok, LANES), 1)
        tok = lax.broadcasted_iota(I32, (n_tok, LANES), 0)
        blk_row = lax.broadcasted_iota(I32, (LANES, LANES), 0)

        def bias_body(n, m):
            onehot = jnp.where(blk_row == n, 1.0, 0.0).astype(BF16)
            chosen = _mm(sel, onehot) > 0.5
            for u in range(ppb):
                p = ppb * n + u
                dist = (past + tok - (p * PAGE_SIZE + lane_t)).astype(F32)
                s3 = sc[p].reshape(A_HEADS, n_tok, LANES) * ATTN_SCALE - slope3 * dist[None]
                s = jnp.where(chosen, s3.reshape(m_rows, LANES), NEG_INF)
                sc[p] = s
                m = jnp.maximum(m, s)
            return m

        m = lax.fori_loop(0, n_blocks, bias_body, jnp.full((m_rows, LANES), M_FLOOR, F32),
                          unroll=math.gcd(n_blocks, 4))
        dist = tok - lane_t
        s3 = sc[n_pages].reshape(A_HEADS, n_tok, LANES) * ATTN_SCALE - slope3 * dist.astype(F32)[None]
        s = jnp.where((dist >= 0)[None], s3, NEG_INF).reshape(m_rows, LANES)
        sc[n_pages] = s
        m = jnp.max(jnp.maximum(m, s), axis=1, keepdims=True)

        def exp_body(p, l):
            e = jnp.exp(sc[p] - m)
            sc[p] = e
            return l + e

        l = lax.fori_loop(0, n_pages + 1, exp_body, jnp.zeros((m_rows, LANES), F32))
        l_ref[...] = jnp.sum(l, axis=1, keepdims=True)

    @pl.when(ph == 1)
    def _():
        @pl.when(g == 0)
        def _():
            acc_ref[...] = jnp.zeros_like(acc_ref)
        a = acc_ref[...]
        for i in range(pp):
            a = a + _nt(sc[g * pp + i].astype(BF16), v_refs[i][...].astype(BF16))
        acc_ref[...] = a

    @pl.when((ph == 1) & (g == n_groups - 1))
    def _():
        a = acc_ref[...] + _nt(sc[n_pages].astype(BF16), vnT_ref[0].astype(BF16))
        o_ref[0] = _merge_heads(a / l_ref[...], A_HEADS, n_tok, HEAD_DIM).astype(o_ref.dtype)


def _block_diag_rows(q):
    r, t, h, dh = q.shape
    qh = jnp.transpose(q, (0, 2, 1, 3))
    eye = jnp.eye(h, dtype=bool)[None, :, None, :, None]
    out = jnp.where(eye, qh[:, :, :, None, :], 0.0)
    return out.reshape(r, h * t, h * dh)


def _new_T(x):
    xt = jnp.transpose(x, (0, 2, 1))
    return jnp.pad(xt, ((0, 0), (0, 0), (0, PAGE_SIZE - x.shape[1])))


def _moba_sample(qa, ka, va, cache_kT, cache_vT, page_table, layer, slopes):
    r, t, _ = qa.shape
    n_pages = page_table.shape[1]
    pp = min(SAMPLE_PAGES_PER_STEP, n_pages)
    n_groups = n_pages // pp
    m_rows = A_HEADS * t
    n_blocks = n_pages * PAGE_SIZE // MOBA_BLOCK
    kern = functools.partial(_moba_sample_kernel, pp=pp, n_groups=n_groups, n_pages=n_pages, n_tok=t,
                             past=n_pages * PAGE_SIZE, n_sel=min(MOBA_TOPK, n_blocks))
    per_req = lambda a: pl.BlockSpec((1,) + a.shape[1:], lambda rr, ph, g, pt: (rr, 0, 0))
    qbd = _block_diag_rows(qa.reshape(r, t, A_HEADS, HEAD_DIM))
    knT, vnT = _new_T(ka), _new_T(va)
    grid_spec = pltpu.PrefetchScalarGridSpec(
        num_scalar_prefetch=1,
        grid=(r, 2, n_groups),
        in_specs=[_smem(), per_req(qbd), per_req(knT), per_req(vnT)]
        + _page_specs(n_pages, A_W, layer, pp, n_groups, 0) + _page_specs(n_pages, A_W, layer, pp, n_groups, 1),
        out_specs=pl.BlockSpec((1, t, A_W), lambda rr, ph, g, pt: (rr, 0, 0)),
        scratch_shapes=[pltpu.VMEM((n_pages + 1, m_rows, LANES), F32), pltpu.VMEM((m_rows, A_W), F32),
                        pltpu.VMEM((m_rows, 1), F32)],
    )
    return pl.pallas_call(
        kern, grid_spec=grid_spec, out_shape=jax.ShapeDtypeStruct((r, t, A_W), F32),
        compiler_params=_params(3), name="moba_sample",
    )(page_table, slopes, qbd, knT, vnT, *([cache_kT] * pp), *([cache_vT] * pp))


def _dsa_sample_scores_kernel(pt_ref, q_ref, iq_ref, wb_ref, iknT_ref, knT_ref, *rest, pp, n_tok):
    i_refs, k_refs = rest[:pp], rest[pp:2 * pp]
    isc_ref, sc_ref, isc_new_ref, sc_new_ref = rest[2 * pp:]
    q = q_ref[0]
    iq2 = jnp.concatenate(_split_bf16(iq_ref[0]), axis=0)
    n_iq = IDX_HEADS * n_tok

    def idx_scores(ikT):
        ik_hi, ik_lo = _split_bf16(ikT)
        r = _mm(iq2, ik_hi)
        r = r[:n_iq] + r[n_iq:] + _mm(iq2[:n_iq], ik_lo)
        out = jnp.zeros((n_tok, LANES), F32)
        for h in range(IDX_HEADS):
            out = out + jnp.maximum(r[h * n_tok:(h + 1) * n_tok], 0.0) * (wb_ref[0, h] * (IDX_W_SCALE * IDX_SCALE))
        return out

    for i in range(pp):
        isc_ref[i, 0] = idx_scores(i_refs[i][...])
        sc_ref[i, 0] = _mm(q, k_refs[i][...].astype(BF16))
    lane = lax.broadcasted_iota(I32, (n_tok, LANES), 1)
    tok = lax.broadcasted_iota(I32, (n_tok, LANES), 0)
    isc_new_ref[0] = jnp.where(lane <= tok, idx_scores(iknT_ref[0]), NEG_INF)
    sc_new_ref[0] = _mm(q, knT_ref[0].astype(BF16))


def _dsa_sample_select_kernel(isc_ref, isc_new_ref, thr_ref, cut_ref, *, n_pages, k_sel, idx_bits):
    r, t, _ = isc_new_ref.shape
    lane = lax.broadcasted_iota(I32, (r, t, LANES), 2)

    def count_at(pred):
        def body(p, c):
            return c + jnp.where(pred(isc_ref[p], p * PAGE_SIZE + lane), 1.0, 0.0)
        c = lax.fori_loop(0, n_pages, body, jnp.zeros((r, t, LANES), F32), unroll=math.gcd(n_pages, 4))
        c = c + jnp.where(pred(isc_new_ref[...], n_pages * PAGE_SIZE + lane), 1.0, 0.0)
        return jnp.sum(c, axis=2, keepdims=True)

    def count(pred):
        return count_at(lambda v, i: pred(v))

    def bounds():
        def body(p, c):
            v = isc_ref[p]
            return jnp.minimum(c[0], v), jnp.maximum(c[1], v)
        v = isc_new_ref[...]
        lo, hi = lax.fori_loop(0, n_pages, body, (jnp.where(v > NEG_INF, v, jnp.inf), v),
                               unroll=math.gcd(n_pages, 4))
        return jnp.min(lo, axis=2, keepdims=True), jnp.max(hi, axis=2, keepdims=True)

    thr, need = _kth_largest(count, bounds, k_sel, (r, t, 1))
    cut = lax.cond(
        jnp.min(need) < ALL_TIES,
        lambda: _index_cut(lambda c: count_at(lambda v, i: (v == thr) & (i < c)), need, (r, t, 1), idx_bits),
        lambda: jnp.full((r, t, 1), 2 ** 31 - 1, I32))
    thr_ref[...] = jnp.broadcast_to(thr, (r, t, LANES))
    cut_ref[...] = jnp.broadcast_to(cut, (r, t, LANES))


def _dsa_sample_attend_kernel(pt_ref, sl_ref, thr_ref, cut_ref, isc_ref, sc_ref, isc_new_ref, sc_new_ref, vnT_ref,
                              *rest, pp, n_groups, n_pages, n_tok, past):
    v_refs = rest[:pp]
    o_ref, m_ref, l_ref, acc_ref = rest[pp:]
    g = pl.program_id(1)
    m_rows = B_HEADS * n_tok
    thr, cut = thr_ref[0], cut_ref[0]
    slope3 = _slopes3(sl_ref, B_HEADS)
    lane = lax.broadcasted_iota(I32, (n_tok, LANES), 1)
    tpos = past + lax.broadcasted_iota(I32, (n_tok, LANES), 0)

    def biased(kk, raw, first):
        idx = first + lane
        chosen = ((kk > thr) | ((kk == thr) & (idx <= cut))) & (idx <= tpos)
        dist = (tpos - idx).astype(F32)
        s3 = raw.reshape(B_HEADS, n_tok, LANES) * ATTN_SCALE - slope3 * dist[None]
        return jnp.where(chosen[None], s3, NEG_INF).reshape(m_rows, LANES)

    def accumulate(tiles, values):
        top = tiles[0]
        for s in tiles[1:]:
            top = jnp.maximum(top, s)
        m_old = m_ref[...]
        m_new = jnp.maximum(m_old, jnp.max(top, axis=1, keepdims=True))
        a = jnp.exp(m_old - m_new)
        tot = jnp.zeros((m_rows, LANES), F32)
        acc = a * acc_ref[...]
        for s, v in zip(tiles, values):
            p = jnp.exp(s - m_new)
            tot = tot + p
            acc = acc + _nt(p.astype(BF16), v.astype(BF16))
        m_ref[...] = m_new
        l_ref[...] = a * l_ref[...] + jnp.sum(tot, axis=1, keepdims=True)
        acc_ref[...] = acc

    @pl.when(g == 0)
    def _():
        m_ref[...] = jnp.full(m_ref.shape, M_FLOOR, F32)
        l_ref[...] = jnp.zeros_like(l_ref)
        acc_ref[...] = jnp.zeros_like(acc_ref)
        accumulate([biased(isc_new_ref[0], sc_new_ref[0], n_pages * PAGE_SIZE)], [vnT_ref[0]])

    accumulate([biased(isc_ref[i, 0], sc_ref[i, 0], (g * pp + i) * PAGE_SIZE) for i in range(pp)],
               [v_refs[i][...] for i in range(pp)])

    @pl.when(g == n_groups - 1)
    def _():
        o_ref[0] = _merge_heads(acc_ref[...] / l_ref[...], B_HEADS, n_tok, HEAD_DIM).astype(o_ref.dtype)


def _dsa_sample(qb, kb, vb, iq, ik, iw, cache_kT, cache_vT, cache_iT, page_table, layer, slopes):
    r, t, _ = qb.shape
    n_pages = page_table.shape[1]
    past = n_pages * PAGE_SIZE
    pp = min(SAMPLE_PAGES_PER_STEP, n_pages)
    n_groups = n_pages // pp
    m_rows = B_HEADS * t
    qbd = _block_diag_rows(qb.reshape(r, t, B_HEADS, HEAD_DIM)).astype(BF16)
    iqs = jnp.transpose(iq.reshape(r, t, IDX_HEADS, IDX_DIM), (0, 2, 1, 3)).reshape(r, IDX_HEADS * t, IDX_DIM)
    wbs = jnp.broadcast_to(jnp.transpose(iw, (0, 2, 1))[..., None], (r, IDX_HEADS, t, LANES))
    iknT, knT, vnT = _new_T(ik), _new_T(kb), _new_T(vb)
    per_req = lambda a: pl.BlockSpec((1,) + a.shape[1:], lambda rr, g, *_: (rr,) + (0,) * (a.ndim - 1))
    pages = lambda width: [pl.BlockSpec((None, None, width, PAGE_SIZE),
                                        functools.partial(lambda rr, g, pt, i: (layer, pt[rr, g * pp + i], 0, 0), i=i))
                           for i in range(pp)]
    paged = lambda rows: pl.BlockSpec((pp, 1, rows, LANES), lambda rr, g, *_: (g, rr, 0, 0))
    isc_shape = jax.ShapeDtypeStruct((n_pages, r, t, LANES), F32)
    sc_shape = jax.ShapeDtypeStruct((n_pages, r, m_rows, LANES), F32)
    isc_new_shape = jax.ShapeDtypeStruct((r, t, LANES), F32)
    sc_new_shape = jax.ShapeDtypeStruct((r, m_rows, LANES), F32)

    isc, sc, isc_new, sc_new = pl.pallas_call(
        functools.partial(_dsa_sample_scores_kernel, pp=pp, n_tok=t),
        grid_spec=pltpu.PrefetchScalarGridSpec(
            num_scalar_prefetch=1, grid=(r, n_groups),
            in_specs=[per_req(qbd), per_req(iqs), per_req(wbs), per_req(iknT), per_req(knT)]
            + pages(IDX_DIM) + pages(B_W),
            out_specs=[paged(t), paged(m_rows), per_req(isc_new_shape), per_req(sc_new_shape)]),
        out_shape=[isc_shape, sc_shape, isc_new_shape, sc_new_shape],
        compiler_params=_params(2), name="dsa_sample_scores",
    )(page_table, qbd, iqs, wbs, iknT, knT, *([cache_iT] * pp), *([cache_kT] * pp))

    whole = lambda a: pl.BlockSpec(a.shape, lambda i: (0,) * a.ndim, pipeline_mode=pl.Buffered(1))
    thr, cut = pl.pallas_call(
        functools.partial(_dsa_sample_select_kernel, n_pages=n_pages, k_sel=min(DSA_TOPK, (past + t) // 4),
                          idx_bits=(past + PAGE_SIZE - 1).bit_length()),
        grid=(1,),
        in_specs=[whole(isc_shape), whole(isc_new_shape)],
        out_specs=[pl.BlockSpec((r, t, LANES), lambda i: (0, 0, 0))] * 2,
        out_shape=[isc_new_shape, jax.ShapeDtypeStruct((r, t, LANES), I32)],
        compiler_params=_params(1), name="dsa_sample_select",
    )(isc, isc_new)

    return pl.pallas_call(
        functools.partial(_dsa_sample_attend_kernel, pp=pp, n_groups=n_groups, n_pages=n_pages, n_tok=t, past=past),
        grid_spec=pltpu.PrefetchScalarGridSpec(
            num_scalar_prefetch=1, grid=(r, n_groups),
            in_specs=[_smem(), per_req(thr), per_req(cut), paged(t), paged(m_rows), per_req(isc_new),
                      per_req(sc_new), per_req(vnT)] + pages(B_W),
            out_specs=pl.BlockSpec((1, t, B_W), lambda rr, g, *_: (rr, 0, 0)),
            scratch_shapes=[pltpu.VMEM((m_rows, 1), F32), pltpu.VMEM((m_rows, 1), F32),
                            pltpu.VMEM((m_rows, B_W), F32)]),
        out_shape=jax.ShapeDtypeStruct((r, t, B_W), F32),
        compiler_params=_params(2), name="dsa_sample_attend",
    )(page_table, slopes, thr, cut, isc, sc, isc_new, sc_new, vnT, *([cache_vT] * pp))


def _alibi_slopes(n):
    return jnp.exp2(-8.0 * jnp.arange(1, n + 1, dtype=F32) / n)


def _heads_T(xT, heads):
    b, _, s = xT.shape
    return jnp.transpose(xT.reshape(b, heads, HEAD_DIM, s), (0, 3, 1, 2))


def _paged_T(cache):
    if cache.ndim == 5:
        d, n, p, h, dh = cache.shape
        return jnp.transpose(cache, (0, 1, 3, 4, 2)).reshape(d, n, h * dh, p)
    return jnp.transpose(cache, (0, 1, 3, 2))


def kernel(x_prompt, x_sample, cache_a_k, cache_a_v, cache_b_k, cache_b_v, cache_idx_k, cache_mem_k, cache_mem_v,
           page_table, mem_prompt, w_in, w_mem_kv, w_out, ln1_g, ln1_b, w_ff1, w_ff2, ln2_g, ln2_b):
    depth = w_in.shape[0]
    bsz, seq, d_model = x_prompt.shape
    r, t, _ = x_sample.shape
    assert seq % KEY_BLOCK == 0 and page_table.shape[1] * PAGE_SIZE % MOBA_BLOCK == 0
    assert page_table.shape[1] * PAGE_SIZE // MOBA_BLOCK <= LANES and seq // MOBA_BLOCK <= LANES
    alpha = (2 * depth) ** 0.25
    slopes_a, slopes_b = _alibi_slopes(A_HEADS), _alibi_slopes(B_HEADS)
    tm_proj = min(512, seq)
    tm_tail = min(512, bsz * seq)

    vec = lambda a, l: a[l].reshape(1, d_model)

    def tail(x2d, oa, ob, om, l, tm, wdtype):
        n = x2d.shape[0]
        return _tail(x2d, oa.reshape(n, A_W), ob.reshape(n, B_W), om.reshape(n, M_W), w_out[l].astype(wdtype),
                     vec(ln1_g, l), vec(ln1_b, l), w_ff1[l].astype(wdtype), w_ff2[l].astype(wdtype),
                     vec(ln2_g, l), vec(ln2_b, l), alpha, tm)

    x = x_prompt
    p_states = []
    for l in range(depth):
        wn, wt = _split_w_in(w_in[l])
        ka, kb, qm, ik, kaT, vaT, kbT, vbT, ikT, qaT, qbT, iqT, iwT = _proj_prompt(x, wn, wt, tm_proj)
        kvT = _mem_kv(mem_prompt, w_mem_kv[l].T.astype(BF16))
        kmT, vmT = kvT[:, :M_W], kvT[:, M_W:]
        o_a = _moba_prompt(qaT, ka, kaT, vaT, slopes_a)
        o_b = _dsa_prompt(qbT, iqT, iwT, ik, kb, vbT, slopes_b)
        o_m = _mem_attn(qm, kmT, vmT, KEY_BLOCK, BF16)
        x = tail(x.reshape(bsz * seq, d_model), o_a, o_b, o_m, l, tm_tail, BF16).reshape(bsz, seq, d_model)
        p_states.append((_heads_T(kaT, A_HEADS), _heads_T(vaT, A_HEADS), _heads_T(kbT, B_HEADS),
                         _heads_T(vbT, B_HEADS), jnp.transpose(ikT, (0, 2, 1)),
                         _heads_T(kmT, M_HEADS), _heads_T(vmT, M_HEADS)))
    y_prompt = x

    caT, cvT = _paged_T(cache_a_k), _paged_T(cache_a_v)
    cbkT, cbvT, ciT = _paged_T(cache_b_k), _paged_T(cache_b_v), _paged_T(cache_idx_k)
    n_mem = cache_mem_k.shape[2]
    mem_T = lambda c: jnp.transpose(c, (0, 2, 3, 1)).reshape(r, M_W, n_mem)
    o = _OFF
    x = x_sample.reshape(r * t, d_model)
    s_states = []
    for l in range(depth):
        p = _matmul(x, w_in[l]).reshape(r, t, -1)
        qa, ka, va, qb, kb, vb, iq, ik, iw, qm = [p[..., o[i]:o[i + 1]] for i in range(10)]
        o_a = _moba_sample(qa, ka, va, caT, cvT, page_table, l, slopes_a)
        o_b = _dsa_sample(qb, kb, vb, iq, ik, iw, cbkT, cbvT, ciT, page_table, l, slopes_b)
        o_m = _mem_attn(qm, mem_T(cache_mem_k[l]), mem_T(cache_mem_v[l]), t, F32)
        x = tail(x, o_a, o_b, o_m, l, r * t, F32)
        s_states.append((ka.reshape(r, t, A_HEADS, HEAD_DIM), va.reshape(r, t, A_HEADS, HEAD_DIM),
                         kb.reshape(r, t, B_HEADS, HEAD_DIM), vb.reshape(r, t, B_HEADS, HEAD_DIM), ik))
    y_sample = x.reshape(r, t, d_model)

    p_out = [jnp.stack(z, axis=0) for z in zip(*p_states)]
    s_out = [jnp.stack(z, axis=0) for z in zip(*s_states)]
    return (y_prompt, y_sample, *p_out, *s_out)
```
